```python
import math
import jax, jax.numpy as jnp
from jax import lax
import numpy as np

D_MODEL = 1024
BATCH = 8
SEQ = 8192
DEPTH = 4

GRID_W = 64
CTX_LEN = 256
N_MIXERS = 3
N_A = (DEPTH + 2) // 3
N_B = (DEPTH + 1) // 3
N_C = DEPTH // 3
N_MOD = 6

SSD_EXPAND = 2
D_INNER = SSD_EXPAND * D_MODEL
SSD_HEADDIM = 64
SSD_HEADS = D_INNER // SSD_HEADDIM
SSD_GROUPS = 4
SSD_HPG = SSD_HEADS // SSD_GROUPS
SSD_STATE = 128
SSD_CONV_W = 5
SSD_CHUNK = 128
SSD_CONV_DIM = D_INNER + 2 * SSD_GROUPS * SSD_STATE
SSD_IN_DIM = D_INNER + SSD_CONV_DIM + 2 * SSD_HEADS

ATTN_HEAD_DIM = 64
ATTN_Q_HEADS = D_MODEL // ATTN_HEAD_DIM
ATTN_KV_HEADS = 4
ATTN_GROUP = ATTN_Q_HEADS // ATTN_KV_HEADS
ATTN_WINDOW = 128
ATTN_BLOCK = 128
ATTN_BAND = ATTN_BLOCK + 2 * ATTN_WINDOW
ATTN_QKV_DIM = (ATTN_Q_HEADS + 2 * ATTN_KV_HEADS) * ATTN_HEAD_DIM
ROPE_BASE = 10000.0
ROPE_FREQS = ATTN_HEAD_DIM // 4

GMLP_WIDTH = 2 * D_MODEL
GMLP_GROUPS = 8
GMLP_GROUP_DIM = GMLP_WIDTH // GMLP_GROUPS
GMLP_CHUNK = 128

FFN_HIDDEN = -(-8 * D_MODEL // (3 * 256)) * 256

kernel_name = "hybrid_ssd_swa_gmlp_diffusion_block"


def rms_norm(x, g, eps=1e-6):
    xf = x.astype(jnp.float32)
    y = xf * lax.rsqrt(jnp.mean(xf * xf, axis=-1, keepdims=True) + eps)
    return (y * g.astype(jnp.float32)).astype(x.dtype)


def layer_norm(x, g, b, eps=1e-5):
    xf = x.astype(jnp.float32)
    mu = jnp.mean(xf, axis=-1, keepdims=True)
    xc = xf - mu
    var = jnp.mean(xc * xc, axis=-1, keepdims=True)
    return (xc * lax.rsqrt(var + eps) * g.astype(jnp.float32) + b.astype(jnp.float32)).astype(x.dtype)


def modulate(h, shift, scale):
    return h * (1 + scale) + shift


def swiglu(h, w_in, w_out):
    gu = h @ w_in
    return (jax.nn.silu(gu[..., :FFN_HIDDEN]) * gu[..., FFN_HIDDEN:]) @ w_out


def axial_rope_tables(row, col, dtype):
    inv = ROPE_BASE ** (-jnp.arange(ROPE_FREQS, dtype=jnp.float32) / ROPE_FREQS)
    ang = jnp.stack([row.astype(jnp.float32)[:, None] * inv, col.astype(jnp.float32)[:, None] * inv], axis=1)
    ang = jnp.repeat(ang[:, :, None, :], 2, axis=2).reshape(row.shape[0], ATTN_HEAD_DIM)
    return jnp.cos(ang).astype(dtype), jnp.sin(ang).astype(dtype)


def apply_axial_rope(x, cos, sin):
    xs = x.reshape(x.shape[:-1] + (2, 2, ROPE_FREQS))
    rot = jnp.stack([-xs[..., 1, :], xs[..., 0, :]], axis=-2).reshape(x.shape)
    return x * cos[:, None, :] + rot * sin[:, None, :]


def depthwise_conv(u, w, b):
    ch = u.shape[-1]
    y = lax.conv_general_dilated(u, w[:, None, :].astype(u.dtype), window_strides=(1,),
                                 padding=[(SSD_CONV_W // 2, SSD_CONV_W // 2)],
                                 dimension_numbers=('NWC', 'WIO', 'NWC'), feature_group_count=ch)
    return y + b


def ssd_scan(xs, dt, A, Bm, Cm, state0):
    f32 = jnp.float32
    xs, dt, Bm, Cm = xs.astype(f32), dt.astype(f32), Bm.astype(f32), Cm.astype(f32)
    b, l = xs.shape[:2]
    nc = l // SSD_CHUNK

    def chunks(t):
        return jnp.moveaxis(t.reshape((b, nc, SSD_CHUNK) + t.shape[2:]), 1, 0)

    mask = jnp.tril(jnp.ones((SSD_CHUNK, SSD_CHUNK), dtype=bool))[None, :, :, None, None]

    def step(state, inp):
        xc, dtc, bc, cc = inp
        a_cum = jnp.cumsum(dtc * A, axis=1)
        seg = a_cum[:, :, None] - a_cum[:, None]
        decay = jnp.exp(jnp.where(mask, seg, -jnp.inf))
        xdt = xc * dtc[..., None]
        cb = jnp.einsum('blgn,bsgn->blsg', cc, bc)
        y = jnp.einsum('blsg,blsgr,bsgrp->blgrp', cb, decay, xdt)
        y = y + jnp.einsum('blgn,bgrpn->blgrp', cc, state) * jnp.exp(a_cum)[..., None]
        w_end = jnp.exp(a_cum[:, -1:] - a_cum)
        state = state * jnp.exp(a_cum[:, -1])[..., None, None] + jnp.einsum('bsgn,bsgr,bsgrp->bgrpn', bc, w_end, xdt)
        return state, y

    final, ys = lax.scan(step, state0.astype(f32), (chunks(xs), chunks(dt), chunks(Bm), chunks(Cm)))
    return jnp.moveaxis(ys, 0, 1).reshape(xs.shape), final


def ssd_mixer(h_ctx, h_lat, w_in, conv_w, conv_b, a_log, dt_bias, d_skip, norm_w, w_out, need_ctx_out):
    A = -jnp.exp(a_log.astype(jnp.float32)).reshape(2, SSD_GROUPS, SSD_HPG)
    d = d_skip.astype(jnp.float32).reshape(SSD_GROUPS, SSD_HPG)[..., None]
    flip = lambda t: jnp.flip(t, axis=1)

    def branch(h):
        b, l, _ = h.shape
        zxbcdt = h @ w_in
        z = zxbcdt[..., :D_INNER]
        xbc = jax.nn.silu(depthwise_conv(zxbcdt[..., D_INNER:D_INNER + SSD_CONV_DIM], conv_w, conv_b))
        dt_raw = zxbcdt[..., D_INNER + SSD_CONV_DIM:].astype(jnp.float32).reshape(b, l, 2, SSD_HEADS)
        dt = jax.nn.softplus(dt_raw + dt_bias.astype(jnp.float32)).reshape(b, l, 2, SSD_GROUPS, SSD_HPG)
        xs = xbc[..., :D_INNER].reshape(b, l, SSD_GROUPS, SSD_HPG, SSD_HEADDIM)
        gn = SSD_GROUPS * SSD_STATE
        Bm = xbc[..., D_INNER:D_INNER + gn].reshape(b, l, SSD_GROUPS, SSD_STATE)
        Cm = xbc[..., D_INNER + gn:].reshape(b, l, SSD_GROUPS, SSD_STATE)
        return z, xs, Bm, Cm, dt

    def run(parts, s_f, s_b):
        z, xs, Bm, Cm, dt = parts
        y_f, fin_f = ssd_scan(xs, dt[:, :, 0], A[0], Bm, Cm, s_f)
        y_b, fin_b = ssd_scan(flip(xs), flip(dt[:, :, 1]), A[1], flip(Bm), flip(Cm), s_b)
        y = y_f + flip(y_b) + xs.astype(jnp.float32) * d
        return y, fin_f, fin_b

    def finish(y, z, dtype):
        b, l = z.shape[:2]
        g = y.reshape(b, l, D_INNER) * jax.nn.silu(z.astype(jnp.float32))
        return rms_norm(g, norm_w).astype(dtype) @ w_out

    ctx_parts = branch(h_ctx)
    zero = jnp.zeros((h_ctx.shape[0], SSD_GROUPS, SSD_HPG, SSD_HEADDIM, SSD_STATE), jnp.float32)
    y_c, s_f, s_b = run(ctx_parts, zero, zero)
    lat_parts = branch(h_lat)
    y_l, _, _ = run(lat_parts, s_f, s_b)
    o_lat = finish(y_l, lat_parts[0], h_lat.dtype)
    o_ctx = finish(y_c, ctx_parts[0], h_ctx.dtype) if need_ctx_out else None
    return o_ctx, o_lat


def window_attn_mixer(h_ctx, h_lat, w_qkv, sink, w_o, cos, sin, need_ctx_out):
    scale = ATTN_HEAD_DIM ** -0.5
    qd, kd = ATTN_Q_HEADS * ATTN_HEAD_DIM, ATTN_KV_HEADS * ATTN_HEAD_DIM
    sink_g = sink.astype(jnp.float32).reshape(1, ATTN_KV_HEADS, ATTN_GROUP, 1, 1)

    def proj(h):
        b, l, _ = h.shape
        qkv = h @ w_qkv
        q = qkv[..., :qd].reshape(b, l, ATTN_Q_HEADS, ATTN_HEAD_DIM)
        k = qkv[..., qd:qd + kd].reshape(b, l, ATTN_KV_HEADS, ATTN_HEAD_DIM)
        v = qkv[..., qd + kd:].reshape(b, l, ATTN_KV_HEADS, ATTN_HEAD_DIM)
        return q, k, v

    def group(q):
        return q.reshape(q.shape[:2] + (ATTN_KV_HEADS, ATTN_GROUP, ATTN_HEAD_DIM))

    q_c, k_c, v_c = proj(h_ctx)
    n_ctx = k_c.shape[1]

    q_l, k_l, v_l = proj(h_lat)
    q_l = group(apply_axial_rope(q_l, cos, sin))
    k_l = apply_axial_rope(k_l, cos, sin)
    b, n = h_lat.shape[:2]
    nb = n // ATTN_BLOCK
    pad = ((0, 0), (ATTN_WINDOW, ATTN_WINDOW), (0, 0), (0, 0))
    k_pad, v_pad = jnp.pad(k_l, pad), jnp.pad(v_l, pad)
    q_blocks = jnp.moveaxis(q_l.reshape((b, nb, ATTN_BLOCK) + q_l.shape[2:]), 1, 0)
    t_idx = jnp.arange(ATTN_BLOCK)[:, None]
    u_idx = jnp.arange(ATTN_BAND)[None, :]

    def attend_block(args):
        j, qb = args
        start = j * ATTN_BLOCK
        kb = lax.dynamic_slice_in_dim(k_pad, start, ATTN_BAND, axis=1)
        vb = lax.dynamic_slice_in_dim(v_pad, start, ATTN_BAND, axis=1)
        pos_k = start - ATTN_WINDOW + u_idx
        valid = (u_idx >= t_idx) & (u_idx <= t_idx + 2 * ATTN_WINDOW) & (pos_k >= 0) & (pos_k < n)
        s_band = jnp.einsum('bqhgd,bkhd->bhgqk', qb, kb).astype(jnp.float32) * scale
        s_band = jnp.where(valid, s_band, -jnp.inf)
        s_ctx = jnp.einsum('bqhgd,bkhd->bhgqk', qb, k_c).astype(jnp.float32) * scale
        sinks = jnp.broadcast_to(sink_g, s_ctx.shape[:-1] + (1,))
        p = jax.nn.softmax(jnp.concatenate([sinks, s_ctx, s_band], axis=-1), axis=-1).astype(v_l.dtype)
        o = jnp.einsum('bhgqk,bkhd->bqhgd', p[..., 1:1 + n_ctx], v_c)
        return o + jnp.einsum('bhgqk,bkhd->bqhgd', p[..., 1 + n_ctx:], vb)

    o_blocks = lax.map(attend_block, (jnp.arange(nb), q_blocks))
    o_lat = jnp.moveaxis(o_blocks, 0, 1).reshape(b, n, qd) @ w_o

    o_ctx = None
    if need_ctx_out:
        qg = group(q_c)
        s = jnp.einsum('bqhgd,bkhd->bhgqk', qg, k_c).astype(jnp.float32) * scale
        sinks = jnp.broadcast_to(sink_g, s.shape[:-1] + (1,))
        p = jax.nn.softmax(jnp.concatenate([sinks, s], axis=-1), axis=-1)[..., 1:].astype(v_c.dtype)
        o_ctx = jnp.einsum('bhgqk,bkhd->bqhgd', p, v_c).reshape(h_ctx.shape[0], n_ctx, qd) @ w_o
    return o_ctx, o_lat


def gmlp_mixer(h_ctx, h_lat, w_in, ln_g, ln_b, w_s, b_s, w_out, need_ctx_out):
    def mix(h):
        b, l, _ = h.shape
        zz = jax.nn.gelu(h @ w_in, approximate=False)
        u, v = zz[..., :GMLP_WIDTH], zz[..., GMLP_WIDTH:]
        v = layer_norm(v, ln_g, ln_b)
        vb = v.reshape(b, l // GMLP_CHUNK, GMLP_CHUNK, GMLP_GROUPS, GMLP_GROUP_DIM)
        sv = jnp.einsum('gts,bcsgd->bctgd', w_s, vb) + b_s.T[None, None, :, :, None]
        return (u * sv.reshape(b, l, GMLP_WIDTH)) @ w_out
    o_lat = mix(h_lat)
    o_ctx = mix(h_ctx) if need_ctx_out else None
    return o_ctx, o_lat


def setup_inputs(seed: int = 0) -> dict:
    key = jax.random.key(seed)
    ks = jax.random.split(key, 32)
    nrm = lambda k, s, sc: jax.random.normal(k, s, jnp.float32) * sc
    D = D_MODEL
    dt0 = jnp.exp(jax.random.uniform(ks[10], (N_A, 2, SSD_HEADS), jnp.float32)
                  * (math.log(0.1) - math.log(0.001)) + math.log(0.001))
    return {
        "x": nrm(ks[0], (BATCH, SEQ, D), 1.0),
        "c": nrm(ks[1], (BATCH, D), 1.0),
        "ctx": nrm(ks[2], (BATCH, CTX_LEN, D), 1.0),
        "c_ctx": nrm(ks[3], (D,), 1.0),
        "w_mod": nrm(ks[4], (DEPTH, D, N_MOD * D), 0.5 * D ** -0.5),
        "b_mod": nrm(ks[5], (DEPTH, N_MOD * D), 0.02),
        "norm_g": 1.0 + nrm(ks[6], (DEPTH, 2, D), 0.02),
        "final_g": 1.0 + nrm(ks[7], (D,), 0.02),
        "ssd_w_in": nrm(ks[8], (N_A, D, SSD_IN_DIM), D ** -0.5),
        "ssd_conv_w": nrm(ks[9], (N_A, SSD_CONV_W, SSD_CONV_DIM), SSD_CONV_W ** -0.5),
        "ssd_conv_b": nrm(ks[11], (N_A, SSD_CONV_DIM), 0.02),
        "ssd_a_log": jnp.log(jax.random.uniform(ks[12], (N_A, 2, SSD_HEADS), jnp.float32, 1.0, 16.0)),
        "ssd_dt_bias": dt0 + jnp.log(-jnp.expm1(-dt0)),
        "ssd_d": 1.0 + nrm(ks[13], (N_A, SSD_HEADS), 0.1),
        "ssd_norm_w": 1.0 + nrm(ks[14], (N_A, D_INNER), 0.02),
        "ssd_w_out": nrm(ks[15], (N_A, D_INNER, D), D_INNER ** -0.5),
        "attn_w_qkv": nrm(ks[16], (N_B, D, ATTN_QKV_DIM), D ** -0.5),
        "attn_sink": nrm(ks[17], (N_B, ATTN_Q_HEADS), 1.0),
        "attn_w_o": nrm(ks[18], (N_B, ATTN_Q_HEADS * ATTN_HEAD_DIM, D), (ATTN_Q_HEADS * ATTN_HEAD_DIM) ** -0.5),
        "gmlp_w_in": nrm(ks[19], (N_C, D, 2 * GMLP_WIDTH), D ** -0.5),
        "gmlp_ln_g": 1.0 + nrm(ks[20], (N_C, GMLP_WIDTH), 0.02),
        "gmlp_ln_b": nrm(ks[21], (N_C, GMLP_WIDTH), 0.02),
        "gmlp_w_s": nrm(ks[22], (N_C, GMLP_GROUPS, GMLP_CHUNK, GMLP_CHUNK), GMLP_CHUNK ** -0.5),
        "gmlp_b_s": 1.0 + nrm(ks[23], (N_C, GMLP_GROUPS, GMLP_CHUNK), 0.02),
        "gmlp_w_out": nrm(ks[24], (N_C, GMLP_WIDTH, D), GMLP_WIDTH ** -0.5),
        "ffn_w_in": nrm(ks[25], (DEPTH, D, 2 * FFN_HIDDEN), D ** -0.5),
        "ffn_w_out": nrm(ks[26], (DEPTH, FFN_HIDDEN, D), FFN_HIDDEN ** -0.5),
    }


def reference(x, c, ctx, c_ctx, w_mod, b_mod, norm_g, final_g,
              ssd_w_in, ssd_conv_w, ssd_conv_b, ssd_a_log, ssd_dt_bias, ssd_d, ssd_norm_w, ssd_w_out,
              attn_w_qkv, attn_sink, attn_w_o,
              gmlp_w_in, gmlp_ln_g, gmlp_ln_b, gmlp_w_s, gmlp_b_s, gmlp_w_out,
              ffn_w_in, ffn_w_out):
    n = x.shape[1]
    rows = n // GRID_W
    row = jnp.repeat(jnp.arange(rows, dtype=jnp.int32), GRID_W, total_repeat_length=n)
    col = jnp.tile(jnp.arange(GRID_W, dtype=jnp.int32), rows)
    cos, sin = axial_rope_tables(row, col, x.dtype)

    h_lat, h_ctx = x, ctx
    for i in range(DEPTH):
        last = i == DEPTH - 1
        kind, j = i % N_MIXERS, i // N_MIXERS
        m_lat = jax.nn.silu(c) @ w_mod[i] + b_mod[i]
        sh1, sc1, g1, sh2, sc2, g2 = jnp.split(m_lat[:, None, :], N_MOD, axis=-1)
        m_ctx = jax.nn.silu(c_ctx) @ w_mod[i] + b_mod[i]
        csh1, csc1, cg1, csh2, csc2, cg2 = jnp.split(m_ctx[None, None, :], N_MOD, axis=-1)

        a_lat = modulate(rms_norm(h_lat, norm_g[i, 0]), sh1, sc1)
        need_ctx_out = not last
        if kind == 0:
            a_ctx = modulate(rms_norm(h_ctx, norm_g[i, 0]), csh1, csc1)
            o_ctx, o_lat = ssd_mixer(a_ctx, a_lat, ssd_w_in[j], ssd_conv_w[j], ssd_conv_b[j], ssd_a_log[j],
                                     ssd_dt_bias[j], ssd_d[j], ssd_norm_w[j], ssd_w_out[j], need_ctx_out)
        elif kind == 1:
            a_ctx = modulate(rms_norm(h_ctx, norm_g[i, 0]), csh1, csc1)
            o_ctx, o_lat = window_attn_mixer(a_ctx, a_lat, attn_w_qkv[j], attn_sink[j], attn_w_o[j],
                                             cos, sin, need_ctx_out)
        else:
            a_ctx = modulate(rms_norm(h_ctx, norm_g[i, 0]), csh1, csc1) if need_ctx_out else None
            o_ctx, o_lat = gmlp_mixer(a_ctx, a_lat, gmlp_w_in[j], gmlp_ln_g[j], gmlp_ln_b[j], gmlp_w_s[j],
                                      gmlp_b_s[j], gmlp_w_out[j], need_ctx_out)

        h_lat = h_lat + g1 * o_lat
        h_lat = h_lat + g2 * swiglu(modulate(rms_norm(h_lat, norm_g[i, 1]), sh2, sc2), ffn_w_in[i], ffn_w_out[i])
        if need_ctx_out:
            h_ctx = h_ctx + cg1 * o_ctx
            h_ctx = h_ctx + cg2 * swiglu(modulate(rms_norm(h_ctx, norm_g[i, 1]), csh2, csc2),
                                         ffn_w_in[i], ffn_w_out[i])
    return rms_norm(h_lat, final_g)
```

```python
import functools
import math

import jax
import jax.numpy as jnp
import numpy as np
from jax import lax
from jax.experimental import pallas as pl
from jax.experimental.pallas import tpu as pltpu

F32 = jnp.float32
BF16 = jnp.bfloat16

N_MOD = 6
GRID_W = 64
SSD_HEADDIM = 64
SSD_GROUPS = 4
SSD_STATE = 128
SSD_CONV_W = 5
SSD_CHUNK = 128
ATTN_HEAD_DIM = 64
ATTN_KV_HEADS = 4
ATTN_WINDOW = 128
ATTN_BLOCK = 128
ROPE_BASE = 10000.0
GMLP_GROUPS = 8
GMLP_CHUNK = 128
RMS_EPS = 1e-6
LN_EPS = 1e-5

LANES = 128
SUBLANES = 8
VMEM_LIMIT = 56 * 1024 * 1024


def _cparams(*sem):
    return pltpu.CompilerParams(dimension_semantics=sem, vmem_limit_bytes=VMEM_LIMIT)


def _silu(x):
    return x * (1.0 / (1.0 + jnp.exp(-x)))


def _norm_mod(x, g, shift, scale):
    y = x * lax.rsqrt(jnp.mean(x * x, axis=-1, keepdims=True) + RMS_EPS)
    return (y * g) * (1.0 + scale) + shift


def _split3(x):
    hi = x.astype(BF16)
    r1 = x - hi.astype(F32)
    mid = r1.astype(BF16)
    lo = (r1 - mid.astype(F32)).astype(BF16)
    return hi, mid, lo


def _mod_kernel(c_ref, w_ref, b_ref, o_ref):
    x = _silu(c_ref[...])
    w = w_ref[...]
    xh = x.astype(BF16)
    xl = (x - xh.astype(F32)).astype(BF16)
    wh = w.astype(BF16)
    wl = (w - wh.astype(F32)).astype(BF16)
    dot = functools.partial(jnp.dot, preferred_element_type=F32)
    o_ref[...] = dot(xh, wh) + (dot(xh, wl) + dot(xl, wh)) + b_ref[...]


def _modulation(cc, w_mod, b_mod):
    depth, d, n = w_mod.shape
    rows = cc.shape[0]
    tn = 1536
    return pl.pallas_call(
        _mod_kernel,
        out_shape=jax.ShapeDtypeStruct((depth, rows, n), F32),
        grid=(depth, n // tn),
        in_specs=[
            pl.BlockSpec((rows, d), lambda l, j: (0, 0)),
            pl.BlockSpec((None, d, tn), lambda l, j: (l, 0, j)),
            pl.BlockSpec((None, 1, tn), lambda l, j: (l, 0, j)),
        ],
        out_specs=pl.BlockSpec((None, rows, tn), lambda l, j: (l, 0, j)),
        compiler_params=_cparams("parallel", "parallel"),
        name="modulation",
    )(cc, w_mod, b_mod.reshape(depth, 1, n))


def _inproj_kernel(x_ref, g_ref, mod_ref, w_ref, *refs, gelu, has_small):
    if has_small:
        ws_ref, o_ref, os_ref, a_scr = refs
    else:
        o_ref, a_scr = refs

    @pl.when(pl.program_id(1) == 0)
    def _():
        a = _norm_mod(x_ref[...], g_ref[...], mod_ref[0:1, :], mod_ref[1:2, :])
        a_scr[...] = a.astype(BF16)
        if has_small:
            os_ref[...] = jnp.dot(a_scr[...], ws_ref[...], preferred_element_type=F32)

    y = jnp.dot(a_scr[...], w_ref[...], preferred_element_type=F32)
    if gelu:
        y = 0.5 * y * (1.0 + lax.erf(y * (1.0 / math.sqrt(2.0))))
    o_ref[...] = y.astype(o_ref.dtype)


def _inproj(h, g, mod, w, *, seq, tm, tn, w_small=None, gelu=False):
    t, d = h.shape
    n = w.shape[1]
    tiles_per_seq = seq // tm
    bm = mod.shape[0]
    mod_idx = (lambda i, j: (i // tiles_per_seq, 0, 0)) if bm > 1 else (lambda i, j: (0, 0, 0))
    in_specs = [
        pl.BlockSpec((tm, d), lambda i, j: (i, 0)),
        pl.BlockSpec((1, d), lambda i, j: (0, 0)),
        pl.BlockSpec((None, 2, d), mod_idx),
        pl.BlockSpec((d, tn), lambda i, j: (0, j)),
    ]
    out_shape = [jax.ShapeDtypeStruct((t, n), F32)]
    out_specs = [pl.BlockSpec((tm, tn), lambda i, j: (i, j))]
    args = [h, g, mod, w]
    if w_small is not None:
        ns = w_small.shape[1]
        in_specs.append(pl.BlockSpec((d, ns), lambda i, j: (0, 0)))
        out_shape.append(jax.ShapeDtypeStruct((t, ns), F32))
        out_specs.append(pl.BlockSpec((tm, ns), lambda i, j: (i, 0)))
        args.append(w_small)
    outs = pl.pallas_call(
        functools.partial(_inproj_kernel, gelu=gelu, has_small=w_small is not None),
        out_shape=out_shape,
        grid=(t // tm, n // tn),
        in_specs=in_specs,
        out_specs=out_specs,
        scratch_shapes=[pltpu.VMEM((tm, d), BF16)],
        compiler_params=_cparams("parallel", "arbitrary"),
        name="inproj",
    )(*args)
    return outs if w_small is not None else outs[0]


def _qkv_kernel(x_ref, g_ref, mod_ref, w_ref, *refs, rope, n_rot, q_cols, scale):
    if rope:
        cos_ref, sa_ref, sb_ref, o_ref = refs
    else:
        (o_ref,) = refs
    a = _norm_mod(x_ref[...], g_ref[...], mod_ref[0:1, :], mod_ref[1:2, :]).astype(BF16)
    y = jnp.dot(a, w_ref[...], preferred_element_type=F32)
    n = y.shape[1]
    if rope:
        cos, sa, sb = cos_ref[...], sa_ref[...], sb_ref[...]
    for s in range(n // LANES):
        ys = y[:, s * LANES:(s + 1) * LANES]
        if rope and s < n_rot:
            ys = ys * cos + pltpu.roll(ys, LANES - 16, 1) * sa + pltpu.roll(ys, 16, 1) * sb
        if s * LANES < q_cols:
            ys = ys * scale
        o_ref[:, s * LANES:(s + 1) * LANES] = ys.astype(o_ref.dtype)


def _qkv_proj(h, g, mod, w, tables, *, seq, tm, q_cols, k_cols):
    t, d = h.shape
    n = w.shape[1]
    tiles_per_seq = seq // tm
    bm = mod.shape[0]
    mod_idx = (lambda i: (i // tiles_per_seq, 0, 0)) if bm > 1 else (lambda i: (0, 0, 0))
    in_specs = [
        pl.BlockSpec((tm, d), lambda i: (i, 0)),
        pl.BlockSpec((1, d), lambda i: (0, 0)),
        pl.BlockSpec((None, 2, d), mod_idx),
        pl.BlockSpec((d, n), lambda i: (0, 0)),
    ]
    args = [h, g, mod, w]
    rope = tables is not None
    if rope:
        for tb in tables:
            in_specs.append(pl.BlockSpec((tm, LANES), lambda i: (i % tiles_per_seq, 0)))
            args.append(tb)
    return pl.pallas_call(
        functools.partial(_qkv_kernel, rope=rope, n_rot=(q_cols + k_cols) // LANES, q_cols=q_cols,
                          scale=ATTN_HEAD_DIM ** -0.5),
        out_shape=jax.ShapeDtypeStruct((t, n), BF16),
        grid=(t // tm,),
        in_specs=in_specs,
        out_specs=pl.BlockSpec((tm, n), lambda i: (i, 0)),
        compiler_params=_cparams("parallel"),
        name="qkv_proj",
    )(*args)


def _outproj_kernel(a_ref, w_ref, h_ref, mod_ref, o_ref):
    y = jnp.dot(a_ref[...], w_ref[...], preferred_element_type=F32)
    o_ref[...] = h_ref[...] + mod_ref[...] * y


def _outproj(a, w, h, gate, *, seq, tm):
    t, k = a.shape
    d = w.shape[1]
    tiles_per_seq = seq // tm
    bm = gate.shape[0]
    mod_idx = (lambda i: (i // tiles_per_seq, 0, 0)) if bm > 1 else (lambda i: (0, 0, 0))
    return pl.pallas_call(
        _outproj_kernel,
        out_shape=jax.ShapeDtypeStruct((t, d), F32),
        grid=(t // tm,),
        in_specs=[
            pl.BlockSpec((tm, k), lambda i: (i, 0)),
            pl.BlockSpec((k, d), lambda i: (0, 0)),
            pl.BlockSpec((tm, d), lambda i: (i, 0)),
            pl.BlockSpec((None, 1, d), mod_idx),
        ],
        out_specs=pl.BlockSpec((tm, d), lambda i: (i, 0)),
        compiler_params=_cparams("parallel"),
        name="outproj",
    )(a, w, h, gate)


def _ffn_kernel(x_ref, g_ref, mod_ref, wg_ref, wu_ref, wo_ref, *refs, final):
    if final:
        fg_ref, o_ref, a_scr, acc_scr = refs
    else:
        o_ref, a_scr, acc_scr = refs
    j = pl.program_id(1)

    @pl.when(j == 0)
    def _():
        a = _norm_mod(x_ref[...], g_ref[...], mod_ref[0:1, :], mod_ref[1:2, :])
        a_scr[...] = a.astype(BF16)

    a = a_scr[...]
    gte = jnp.dot(a, wg_ref[...], preferred_element_type=F32)
    up = jnp.dot(a, wu_ref[...], preferred_element_type=F32)
    act = (_silu(gte) * up).astype(BF16)
    part = jnp.dot(act, wo_ref[...], preferred_element_type=F32)

    @pl.when(j == 0)
    def _():
        acc_scr[...] = part

    @pl.when(j > 0)
    def _():
        acc_scr[...] += part

    @pl.when(j == pl.num_programs(1) - 1)
    def _():
        y = x_ref[...] + mod_ref[2:3, :] * acc_scr[...]
        if final:
            y = y * lax.rsqrt(jnp.mean(y * y, axis=-1, keepdims=True) + RMS_EPS) * fg_ref[...]
        o_ref[...] = y


def _ffn(h, g, mod, w_in, w_out, *, seq, tm, th, final_g=None):
    t, d = h.shape
    hidden = w_out.shape[0]
    nh = hidden // th
    tiles_per_seq = seq // tm
    bm = mod.shape[0]
    mod_idx = (lambda i, j: (i // tiles_per_seq, 0, 0)) if bm > 1 else (lambda i, j: (0, 0, 0))
    in_specs = [
        pl.BlockSpec((tm, d), lambda i, j: (i, 0)),
        pl.BlockSpec((1, d), lambda i, j: (0, 0)),
        pl.BlockSpec((None, 3, d), mod_idx),
        pl.BlockSpec((d, th), lambda i, j: (0, j)),
        pl.BlockSpec((d, th), lambda i, j: (0, nh + j)),
        pl.BlockSpec((th, d), lambda i, j: (j, 0)),
    ]
    args = [h, g, mod, w_in, w_in, w_out]
    if final_g is not None:
        in_specs.append(pl.BlockSpec((1, d), lambda i, j: (0, 0)))
        args.append(final_g)
    return pl.pallas_call(
        functools.partial(_ffn_kernel, final=final_g is not None),
        out_shape=jax.ShapeDtypeStruct((t, d), F32),
        grid=(t // tm, nh),
        in_specs=in_specs,
        out_specs=pl.BlockSpec((tm, d), lambda i, j: (i, 0)),
        scratch_shapes=[pltpu.VMEM((tm, d), BF16), pltpu.VMEM((tm, d), F32)],
        compiler_params=_cparams("parallel", "arbitrary"),
        name="ffn",
    )(*args)


def _conv_kernel(main_ref, prev_ref, next_ref, w_ref, b_ref, o_ref, ext_scr, *, tiles_per_seq, tm):
    pos = pl.program_id(0) % tiles_per_seq
    half = SSD_CONV_W // 2
    ext_scr[0:SUBLANES, :] = jnp.where(pos == 0, 0.0, prev_ref[...])
    ext_scr[SUBLANES:SUBLANES + tm, :] = main_ref[...]
    ext_scr[SUBLANES + tm:, :] = jnp.where(pos == tiles_per_seq - 1, 0.0, next_ref[...])
    acc = jnp.broadcast_to(b_ref[...], (tm, b_ref.shape[1]))
    for k in range(SSD_CONV_W):
        acc = acc + w_ref[k:k + 1, :] * ext_scr[SUBLANES - half + k:SUBLANES - half + k + tm, :]
    o_ref[...] = _silu(acc)


def _conv_silu(zx, conv_w, conv_b, *, seq, tm, col0, tc):
    t = zx.shape[0]
    c = conv_w.shape[1]
    tiles_per_seq = seq // tm
    cb0 = col0 // tc
    rb = tm // SUBLANES
    last_rb = t // SUBLANES - 1
    return pl.pallas_call(
        functools.partial(_conv_kernel, tiles_per_seq=tiles_per_seq, tm=tm),
        out_shape=jax.ShapeDtypeStruct((t, c), F32),
        grid=(t // tm, c // tc),
        in_specs=[
            pl.BlockSpec((tm, tc), lambda i, j: (i, cb0 + j)),
            pl.BlockSpec((SUBLANES, tc), lambda i, j: (jnp.maximum(i * rb - 1, 0), cb0 + j)),
            pl.BlockSpec((SUBLANES, tc), lambda i, j: (jnp.minimum((i + 1) * rb, last_rb), cb0 + j)),
            pl.BlockSpec((SSD_CONV_W, tc), lambda i, j: (0, j)),
            pl.BlockSpec((1, tc), lambda i, j: (0, j)),
        ],
        out_specs=pl.BlockSpec((tm, tc), lambda i, j: (i, j)),
        scratch_shapes=[pltpu.VMEM((tm + 2 * SUBLANES, tc), F32)],
        compiler_params=_cparams("parallel", "parallel"),
        name="ssd_conv",
    )(zx, zx, zx, conv_w, conv_b)


def _ssd_kernel(x_ref, b_ref, c_ref, dt_ref, bias_ref, alog_ref, s0_ref, *refs,
                reverse, finish, heads, col_off):
    if finish:
        yf_ref, z_ref, d_ref, nw_ref, o_ref, sfin_ref, state, xw_scr, y_scr = refs
    else:
        o_ref, sfin_ref, state, xw_scr, y_scr = refs
    L = SSD_CHUNK
    hd = SSD_HEADDIM
    hpg = heads // SSD_GROUPS
    gw = hpg * hd
    j = pl.program_id(1)

    @pl.when(j == 0)
    def _():
        state[...] = s0_ref[...]

    row = lax.broadcasted_iota(jnp.int32, (L, L), 0)
    col = lax.broadcasted_iota(jnp.int32, (L, L), 1)
    mask = (col >= row) if reverse else (col <= row)
    tri = jnp.where(mask, 1.0, 0.0).astype(BF16)

    v = dt_ref[...] + bias_ref[...]
    dt = jnp.maximum(v, 0.0) + jnp.log1p(jnp.exp(-jnp.abs(v)))
    dta = dt * (-jnp.exp(alog_ref[...]))
    p1, p2, p3 = _split3(dta)
    dot = functools.partial(jnp.dot, preferred_element_type=F32)
    a_cum = dot(tri, p1) + (dot(tri, p2) + dot(tri, p3))
    a_cum_t = a_cum.T
    a_tot = a_cum[0:1, :] if reverse else a_cum[L - 1:L, :]
    w_end = jnp.exp(a_tot - a_cum)
    e_cum = jnp.exp(a_cum)
    e_tot = jnp.exp(a_tot)

    for g in range(SSD_GROUPS):
        bg = b_ref[:, g * SSD_STATE:(g + 1) * SSD_STATE]
        cg = c_ref[:, g * SSD_STATE:(g + 1) * SSD_STATE].astype(BF16)
        cb = lax.dot_general(cg, bg.astype(BF16), (((1,), (1,)), ((), ())), preferred_element_type=F32)
        bg_t = bg.T.astype(BF16)
        y_in = dot(cg, state[:, g * gw:(g + 1) * gw].astype(BF16))
        for r in range(hpg):
            h = g * hpg + r
            cidx = col_off + h
            lo, hi = h * hd, (h + 1) * hd
            seg = a_cum[:, cidx:cidx + 1] - a_cum_t[cidx:cidx + 1, :]
            decay = jnp.exp(jnp.where(mask, seg, -jnp.inf))
            m = (cb * decay).astype(BF16)
            xdt = x_ref[:, lo:hi] * dt[:, cidx:cidx + 1]
            y_scr[:, lo:hi] = dot(m, xdt.astype(BF16)) + y_in[:, r * hd:(r + 1) * hd] * e_cum[:, cidx:cidx + 1]
            xw_scr[:, lo:hi] = xdt * w_end[:, cidx:cidx + 1]
        upd = dot(bg_t, xw_scr[:, g * gw:(g + 1) * gw].astype(BF16))
        for r in range(hpg):
            h = g * hpg + r
            cidx = col_off + h
            lo, hi = h * hd, (h + 1) * hd
            state[:, lo:hi] = state[:, lo:hi] * e_tot[:, cidx:cidx + 1] + upd[:, r * hd:(r + 1) * hd]

    if finish:
        y = y_scr[...] + yf_ref[...] + x_ref[...] * d_ref[...]
        gated = y * _silu(z_ref[...])
        gn = gated * lax.rsqrt(jnp.mean(gated * gated, axis=-1, keepdims=True) + RMS_EPS) * nw_ref[...]
        o_ref[...] = gn.astype(o_ref.dtype)
    else:
        o_ref[...] = y_scr[...]

    @pl.when(j == pl.num_programs(1) - 1)
    def _():
        sfin_ref[...] = state[...]


def _ssd_scan(xc, dt_raw, dt_bias, a_log, state0, *, batch, seq, reverse, d_inner, finish_args=None):
    t = xc.shape[0]
    L = SSD_CHUNK
    nc = seq // L
    heads = d_inner // SSD_HEADDIM
    gn = SSD_GROUPS * SSD_STATE
    finish = finish_args is not None

    def rows(b, j):
        return b * nc + ((nc - 1 - j) if reverse else j)

    in_specs = [
        pl.BlockSpec((L, d_inner), lambda b, j: (rows(b, j), 0)),
        pl.BlockSpec((L, gn), lambda b, j: (rows(b, j), d_inner // gn)),
        pl.BlockSpec((L, gn), lambda b, j: (rows(b, j), d_inner // gn + 1)),
        pl.BlockSpec((L, LANES), lambda b, j: (rows(b, j), 0)),
        pl.BlockSpec((1, LANES), lambda b, j: (0, 0)),
        pl.BlockSpec((1, LANES), lambda b, j: (0, 0)),
        pl.BlockSpec((None, SSD_STATE, d_inner), lambda b, j: (b, 0, 0)),
    ]
    args = [xc, xc, xc, dt_raw, dt_bias, a_log, state0]
    if finish:
        y_f, zx, d_exp, norm_w = finish_args
        in_specs += [
            pl.BlockSpec((L, d_inner), lambda b, j: (rows(b, j), 0)),
            pl.BlockSpec((L, d_inner), lambda b, j: (rows(b, j), 0)),
            pl.BlockSpec((1, d_inner), lambda b, j: (0, 0)),
            pl.BlockSpec((1, d_inner), lambda b, j: (0, 0)),
        ]
        args += [y_f, zx, d_exp, norm_w]
    out_dtype = BF16 if finish else F32
    return pl.pallas_call(
        functools.partial(_ssd_kernel, reverse=reverse, finish=finish, heads=heads,
                          col_off=heads if reverse else 0),
        out_shape=[jax.ShapeDtypeStruct((t, d_inner), out_dtype),
                   jax.ShapeDtypeStruct((batch, SSD_STATE, d_inner), F32)],
        grid=(batch, nc),
        in_specs=in_specs,
        out_specs=[pl.BlockSpec((L, d_inner), lambda b, j: (rows(b, j), 0)),
                   pl.BlockSpec((None, SSD_STATE, d_inner), lambda b, j: (b, 0, 0))],
        scratch_shapes=[pltpu.VMEM((SSD_STATE, d_inner), F32),
                        pltpu.VMEM((L, d_inner), F32),
                        pltpu.VMEM((L, d_inner), F32)],
        compiler_params=_cparams("parallel", "arbitrary"),
        name="ssd_scan_bwd" if reverse else "ssd_scan_fwd",
    )(*args)


def _attn_kernel(sink_ref, q_ref, kc_ref, vc_ref, *refs, band, q_heads):
    if band:
        kp_ref, k0_ref, kn_ref, vp_ref, v0_ref, vn_ref, o_ref = refs
    else:
        (o_ref,) = refs
    hd = ATTN_HEAD_DIM
    group = q_heads // ATTN_KV_HEADS
    blk = q_ref.shape[0]
    dot = functools.partial(jnp.dot, preferred_element_type=F32)
    dot_t = lambda a, b: lax.dot_general(a, b, (((1,), (1,)), ((), ())), preferred_element_type=F32)
    if band:
        j = pl.program_id(1)
        t_idx = lax.broadcasted_iota(jnp.int32, (blk, blk), 0)
        u_idx = lax.broadcasted_iota(jnp.int32, (blk, blk), 1)
        ok_prev = u_idx >= t_idx + jnp.where(j > 0, 0, 2 * blk)
        ok_next = u_idx + jnp.where(j < pl.num_programs(1) - 1, 0, 2 * blk) <= t_idx
    for h in range(q_heads):
        kv = h // group
        ksl = slice(kv * hd, (kv + 1) * hd)
        q = q_ref[:, h * hd:(h + 1) * hd]
        sink = sink_ref[h]
        s_c = dot_t(q, kc_ref[:, ksl])
        m = jnp.maximum(jnp.max(s_c, axis=-1, keepdims=True), sink)
        if band:
            s_p = jnp.where(ok_prev, dot_t(q, kp_ref[:, ksl]), -jnp.inf)
            s_0 = dot_t(q, k0_ref[:, ksl])
            s_n = jnp.where(ok_next, dot_t(q, kn_ref[:, ksl]), -jnp.inf)
            m = jnp.maximum(m, jnp.max(s_p, axis=-1, keepdims=True))
            m = jnp.maximum(m, jnp.max(s_0, axis=-1, keepdims=True))
            m = jnp.maximum(m, jnp.max(s_n, axis=-1, keepdims=True))
        p_c = jnp.exp(s_c - m)
        den = jnp.exp(sink - m) + jnp.sum(p_c, axis=-1, keepdims=True)
        if band:
            p_p = jnp.exp(s_p - m)
            p_0 = jnp.exp(s_0 - m)
            p_n = jnp.exp(s_n - m)
            den = den + (jnp.sum(p_p, axis=-1, keepdims=True) + jnp.sum(p_0, axis=-1, keepdims=True)
                         + jnp.sum(p_n, axis=-1, keepdims=True))
        inv = 1.0 / den
        o = dot((p_c * inv).astype(BF16), vc_ref[:, ksl])
        if band:
            o = o + dot((p_p * inv).astype(BF16), vp_ref[:, ksl])
            o = o + dot((p_0 * inv).astype(BF16), v0_ref[:, ksl])
            o = o + dot((p_n * inv).astype(BF16), vn_ref[:, ksl])
        o_ref[:, h * hd:(h + 1) * hd] = o.astype(o_ref.dtype)


def _attention(sink, qkv, qkv_ctx, *, batch, seq, n_ctx, q_heads, band):
    blk = ATTN_BLOCK
    nb = seq // blk
    qd = q_heads * ATTN_HEAD_DIM
    kd = ATTN_KV_HEADS * ATTN_HEAD_DIM
    kcol, vcol = qd // kd, qd // kd + 1
    in_specs = [
        pl.BlockSpec(memory_space=pltpu.SMEM),
        pl.BlockSpec((blk, qd), lambda b, j: (b * nb + j, 0)),
        pl.BlockSpec((n_ctx, kd), lambda b, j: (b, kcol)),
        pl.BlockSpec((n_ctx, kd), lambda b, j: (b, vcol)),
    ]
    args = [sink, qkv, qkv_ctx, qkv_ctx]
    if band:
        prev = lambda b, j: b * nb + jnp.maximum(j - 1, 0)
        cur = lambda b, j: b * nb + j
        nxt = lambda b, j: b * nb + jnp.minimum(j + 1, nb - 1)
        for colb in (kcol, vcol):
            for f in (prev, cur, nxt):
                in_specs.append(pl.BlockSpec((blk, kd), functools.partial(lambda b, j, f, colb: (f(b, j), colb),
                                                                          f=f, colb=colb)))
                args.append(qkv)
    return pl.pallas_call(
        functools.partial(_attn_kernel, band=band, q_heads=q_heads),
        out_shape=jax.ShapeDtypeStruct((batch * seq, qd), BF16),
        grid=(batch, nb),
        in_specs=in_specs,
        out_specs=pl.BlockSpec((blk, qd), lambda b, j: (b * nb + j, 0)),
        compiler_params=_cparams("parallel", "parallel"),
        name="window_attn" if band else "ctx_attn",
    )(*args)


def _gmlp_kernel(u_ref, v_ref, lg_ref, lb_ref, ws_ref, bs_ref, o_ref):
    v = v_ref[...]
    mu = jnp.mean(v, axis=-1, keepdims=True)
    vc = v - mu
    var = jnp.mean(vc * vc, axis=-1, keepdims=True)
    vn = (vc * lax.rsqrt(var + LN_EPS) * lg_ref[...] + lb_ref[...]).astype(BF16)
    gd = v.shape[1] // GMLP_GROUPS
    for g in range(GMLP_GROUPS):
        sv = jnp.dot(ws_ref[g], vn[:, g * gd:(g + 1) * gd], preferred_element_type=F32) + bs_ref[:, g:g + 1]
        o_ref[:, g * gd:(g + 1) * gd] = (u_ref[:, g * gd:(g + 1) * gd] * sv).astype(o_ref.dtype)


def _gmlp_spatial(zz, ln_g, ln_b, w_s, b_s_t):
    t = zz.shape[0]
    width = zz.shape[1] // 2
    ch = GMLP_CHUNK
    return pl.pallas_call(
        _gmlp_kernel,
        out_shape=jax.ShapeDtypeStruct((t, width), BF16),
        grid=(t // ch,),
        in_specs=[
            pl.BlockSpec((ch, width), lambda i: (i, 0)),
            pl.BlockSpec((ch, width), lambda i: (i, 1)),
            pl.BlockSpec((1, width), lambda i: (0, 0)),
            pl.BlockSpec((1, width), lambda i: (0, 0)),
            pl.BlockSpec((GMLP_GROUPS, ch, ch), lambda i: (0, 0, 0)),
            pl.BlockSpec((ch, GMLP_GROUPS), lambda i: (0, 0)),
        ],
        out_specs=pl.BlockSpec((ch, width), lambda i: (i, 0)),
        compiler_params=_cparams("parallel"),
        name="gmlp_spatial",
    )(zz, zz, ln_g, ln_b, w_s, b_s_t)


def _rope_tables(n):
    freqs = ATTN_HEAD_DIM // 4
    rows = n // GRID_W
    row = jnp.repeat(jnp.arange(rows, dtype=jnp.int32), GRID_W, total_repeat_length=n)
    col = jnp.tile(jnp.arange(GRID_W, dtype=jnp.int32), rows)
    inv = ROPE_BASE ** (-jnp.arange(freqs, dtype=F32) / freqs)
    ang = jnp.stack([row.astype(F32)[:, None] * inv, col.astype(F32)[:, None] * inv], axis=1)
    ang = jnp.repeat(ang[:, :, None, :], 2, axis=2).reshape(n, ATTN_HEAD_DIM)
    cos, sin = jnp.cos(ang), jnp.sin(ang)
    reps = LANES // ATTN_HEAD_DIM
    cos, sin = jnp.tile(cos, (1, reps)), jnp.tile(sin, (1, reps))
    first_half = jnp.asarray((np.arange(LANES) // freqs) % 2 == 0)[None, :]
    return cos, jnp.where(first_half, -sin, 0.0), jnp.where(first_half, 0.0, sin)


def _ssd_mixer(streams, norm_g, w_in, conv_w, conv_b, a_log, dt_bias, d_skip, norm_w, w_out, need_ctx_out):
    d_inner = norm_w.shape[0]
    conv_dim = conv_w.shape[1]
    heads = d_inner // SSD_HEADDIM
    w_main = w_in[:, :d_inner + conv_dim].astype(BF16)
    w_dt = jnp.pad(w_in[:, d_inner + conv_dim:], ((0, 0), (0, LANES - 2 * heads))).astype(BF16)
    bias = jnp.pad(dt_bias.reshape(1, 2 * heads), ((0, 0), (0, LANES - 2 * heads)))
    alog = jnp.pad(a_log.reshape(1, 2 * heads), ((0, 0), (0, LANES - 2 * heads)))
    d_exp = jnp.repeat(d_skip, SSD_HEADDIM).reshape(1, d_inner)
    nw = norm_w.reshape(1, d_inner)
    w_out = w_out.astype(BF16)
    conv_b = conv_b.reshape(1, conv_dim)

    outs = []
    s_f = s_b = None
    for name, s in streams:
        if s_f is None:
            s_f = s_b = jnp.zeros((s["batch"], SSD_STATE, d_inner), F32)
        zx, dt_raw = _inproj(s["h"], norm_g, s["mod"][:, 0:2], w_main, seq=s["seq"], tm=s["tm"], tn=1024,
                             w_small=w_dt)
        xc = _conv_silu(zx, conv_w, conv_b, seq=s["seq"], tm=s["tm"], col0=d_inner, tc=1024)
        y_f, s_f = _ssd_scan(xc, dt_raw, bias, alog, s_f, batch=s["batch"], seq=s["seq"], reverse=False,
                             d_inner=d_inner)
        gn, s_b = _ssd_scan(xc, dt_raw, bias, alog, s_b, batch=s["batch"], seq=s["seq"], reverse=True,
                            d_inner=d_inner, finish_args=(y_f, zx, d_exp, nw))
        if name == "ctx" and not need_ctx_out:
            outs.append(None)
        else:
            outs.append(_outproj(gn, w_out, s["h"], s["mod"][:, 2:3], seq=s["seq"], tm=s["tm"]))
    return outs


def _attn_mixer(streams, norm_g, w_qkv, sink, w_o, tables, need_ctx_out):
    (_, sc), (_, sl) = streams
    q_heads = sink.shape[0]
    qd = q_heads * ATTN_HEAD_DIM
    kd = ATTN_KV_HEADS * ATTN_HEAD_DIM
    w_qkv = w_qkv.astype(BF16)
    w_o = w_o.astype(BF16)
    qkv_c = _qkv_proj(sc["h"], norm_g, sc["mod"][:, 0:2], w_qkv, None, seq=sc["seq"], tm=sc["tm"],
                      q_cols=qd, k_cols=kd)
    qkv_l = _qkv_proj(sl["h"], norm_g, sl["mod"][:, 0:2], w_qkv, tables, seq=sl["seq"], tm=sl["tm"],
                      q_cols=qd, k_cols=kd)
    o_l = _attention(sink, qkv_l, qkv_c, batch=sl["batch"], seq=sl["seq"], n_ctx=sc["seq"], q_heads=q_heads,
                     band=True)
    h_l = _outproj(o_l, w_o, sl["h"], sl["mod"][:, 2:3], seq=sl["seq"], tm=sl["tm"])
    h_c = None
    if need_ctx_out:
        o_c = _attention(sink, qkv_c, qkv_c, batch=sc["batch"], seq=sc["seq"], n_ctx=sc["seq"], q_heads=q_heads,
                         band=False)
        h_c = _outproj(o_c, w_o, sc["h"], sc["mod"][:, 2:3], seq=sc["seq"], tm=sc["tm"])
    return [h_c, h_l]


def _gmlp_mixer(streams, norm_g, w_in, ln_g, ln_b, w_s, b_s, w_out, need_ctx_out):
    width = ln_g.shape[0]
    w_in = w_in.astype(BF16)
    w_out = w_out.astype(BF16)
    w_s = w_s.astype(BF16)
    outs = []
    for name, s in streams:
        if name == "ctx" and not need_ctx_out:
            outs.append(None)
            continue
        zz = _inproj(s["h"], norm_g, s["mod"][:, 0:2], w_in, seq=s["seq"], tm=s["tm"], tn=1024, gelu=True)
        gm = _gmlp_spatial(zz, ln_g.reshape(1, width), ln_b.reshape(1, width), w_s, b_s.T)
        outs.append(_outproj(gm, w_out, s["h"], s["mod"][:, 2:3], seq=s["seq"], tm=s["tm"]))
    return outs


def kernel(x, c, ctx, c_ctx, w_mod, b_mod, norm_g, final_g, ssd_w_in, ssd_conv_w, ssd_conv_b, ssd_a_log, ssd_dt_bias, ssd_d, ssd_norm_w, ssd_w_out, attn_w_qkv, attn_sink, attn_w_o, gmlp_w_in, gmlp_ln_g, gmlp_ln_b, gmlp_w_s, gmlp_b_s, gmlp_w_out, ffn_w_in, ffn_w_out):
    batch, seq, d = x.shape
    n_ctx = ctx.shape[1]
    depth = w_mod.shape[0]
    hidden = ffn_w_out.shape[1]

    pad_rows = -(batch + 1) % 16
    cc = jnp.concatenate([c, c_ctx[None, :], jnp.zeros((pad_rows, d), F32)], axis=0)
    mods = _modulation(cc, w_mod, b_mod)
    tables = _rope_tables(seq)

    h_lat = x.reshape(batch * seq, d)
    h_ctx = ctx.reshape(batch * n_ctx, d)
    tm_lat = 512 if seq % 512 == 0 else 256
    tm_ctx = 256
    th = hidden // 2

    for i in range(depth):
        last = i == depth - 1
        kind, j = i % 3, i // 3
        mod_lat = mods[i, :batch].reshape(batch, N_MOD, d)
        mod_ctx = mods[i, batch:batch + 1].reshape(1, N_MOD, d)
        g1 = norm_g[i, 0].reshape(1, d)
        g2 = norm_g[i, 1].reshape(1, d)
        streams = [
            ("ctx", dict(h=h_ctx, mod=mod_ctx, batch=batch, seq=n_ctx, tm=tm_ctx)),
            ("lat", dict(h=h_lat, mod=mod_lat, batch=batch, seq=seq, tm=tm_lat)),
        ]
        need_ctx_out = not last
        if kind == 0:
            h_ctx_new, h_lat = _ssd_mixer(streams, g1, ssd_w_in[j], ssd_conv_w[j], ssd_conv_b[j], ssd_a_log[j],
                                          ssd_dt_bias[j], ssd_d[j], ssd_norm_w[j], ssd_w_out[j], need_ctx_out)
        elif kind == 1:
            h_ctx_new, h_lat = _attn_mixer(streams, g1, attn_w_qkv[j], attn_sink[j], attn_w_o[j], tables,
                                           need_ctx_out)
        else:
            h_ctx_new, h_lat = _gmlp_mixer(streams, g1, gmlp_w_in[j], gmlp_ln_g[j], gmlp_ln_b[j], gmlp_w_s[j],
                                           gmlp_b_s[j], gmlp_w_out[j], need_ctx_out)

        w_in = ffn_w_in[i].astype(BF16)
        w_out = ffn_w_out[i].astype(BF16)
        h_lat = _ffn(h_lat, g2, mod_lat[:, 3:6], w_in, w_out, seq=seq, tm=tm_lat, th=th,
                     final_g=final_g.reshape(1, d) if last else None)
        if need_ctx_out:
            h_ctx = _ffn(h_ctx_new, g2, mod_ctx[:, 3:6], w_in, w_out, seq=n_ctx, tm=tm_ctx, th=th)
    return h_lat.reshape(batch, seq, d)
```

```python
import functools
import math

import jax
import jax.numpy as jnp
import numpy as np
from jax import lax
from jax.experimental import pallas as pl
from jax.experimental.pallas import tpu as pltpu

F32 = jnp.float32
BF16 = jnp.bfloat16

N_MOD = 6
GRID_W = 64
SSD_HEADDIM = 64
SSD_GROUPS = 4
SSD_STATE = 128
SSD_CONV_W = 5
SSD_CHUNK = 128
ATTN_HEAD_DIM = 64
ATTN_KV_HEADS = 4
ATTN_WINDOW = 128
ATTN_BLOCK = 128
ROPE_BASE = 10000.0
GMLP_GROUPS = 8
GMLP_CHUNK = 128
RMS_EPS = 1e-6
LN_EPS = 1e-5

LANES = 128
SUBLANES = 8
VMEM_LIMIT = 56 * 1024 * 1024


def _cparams(*sem):
    return pltpu.CompilerParams(dimension_semantics=sem, vmem_limit_bytes=VMEM_LIMIT)


def _silu(x):
    return x * (1.0 / (1.0 + jnp.exp(-x)))


def _norm_mod(x, g, shift, scale):
    y = x * lax.rsqrt(jnp.mean(x * x, axis=-1, keepdims=True) + RMS_EPS)
    return (y * g) * (1.0 + scale) + shift


def _split3(x):
    hi = x.astype(BF16)
    r1 = x - hi.astype(F32)
    mid = r1.astype(BF16)
    lo = (r1 - mid.astype(F32)).astype(BF16)
    return hi, mid, lo


def _mod_kernel(c_ref, w_ref, b_ref, o_ref):
    x = _silu(c_ref[...])
    w = w_ref[...]
    xh = x.astype(BF16)
    xl = (x - xh.astype(F32)).astype(BF16)
    wh = w.astype(BF16)
    wl = (w - wh.astype(F32)).astype(BF16)
    dot = functools.partial(jnp.dot, preferred_element_type=F32)
    o_ref[...] = dot(xh, wh) + (dot(xh, wl) + dot(xl, wh)) + b_ref[...]


def _modulation(cc, w_mod, b_mod):
    depth, d, n = w_mod.shape
    rows = cc.shape[0]
    tn = 1536
    return pl.pallas_call(
        _mod_kernel,
        out_shape=jax.ShapeDtypeStruct((depth, rows, n), F32),
        grid=(depth, n // tn),
        in_specs=[
            pl.BlockSpec((rows, d), lambda l, j: (0, 0)),
            pl.BlockSpec((None, d, tn), lambda l, j: (l, 0, j)),
            pl.BlockSpec((None, 1, tn), lambda l, j: (l, 0, j)),
        ],
        out_specs=pl.BlockSpec((None, rows, tn), lambda l, j: (l, 0, j)),
        compiler_params=_cparams("parallel", "parallel"),
        name="modulation",
    )(cc, w_mod, b_mod.reshape(depth, 1, n))


def _inproj_kernel(x_ref, g_ref, mod_ref, w_ref, *refs, gelu, has_small, chunk):
    if has_small:
        ws_ref, o_ref, os_ref = refs
    else:
        (o_ref,) = refs
    a = _norm_mod(x_ref[...], g_ref[...], mod_ref[0:1, :], mod_ref[1:2, :]).astype(BF16)
    if has_small:
        os_ref[...] = jnp.dot(a, ws_ref[...], preferred_element_type=F32)
    for c0 in range(0, w_ref.shape[1], chunk):
        y = jnp.dot(a, w_ref[:, c0:c0 + chunk], preferred_element_type=F32)
        if gelu:
            y = 0.5 * y * (1.0 + lax.erf(y * (1.0 / math.sqrt(2.0))))
        o_ref[:, c0:c0 + chunk] = y.astype(o_ref.dtype)


def _inproj(h, g, mod, w, *, seq, tm, chunk, w_small=None, gelu=False):
    t, d = h.shape
    n = w.shape[1]
    tiles_per_seq = seq // tm
    bm = mod.shape[0]
    mod_idx = (lambda i: (i // tiles_per_seq, 0, 0)) if bm > 1 else (lambda i: (0, 0, 0))
    in_specs = [
        pl.BlockSpec((tm, d), lambda i: (i, 0)),
        pl.BlockSpec((1, d), lambda i: (0, 0)),
        pl.BlockSpec((None, 2, d), mod_idx),
        pl.BlockSpec((d, n), lambda i: (0, 0), pipeline_mode=pl.Buffered(1)),
    ]
    out_shape = [jax.ShapeDtypeStruct((t, n), F32)]
    out_specs = [pl.BlockSpec((tm, n), lambda i: (i, 0))]
    args = [h, g, mod, w]
    if w_small is not None:
        ns = w_small.shape[1]
        in_specs.append(pl.BlockSpec((d, ns), lambda i: (0, 0)))
        out_shape.append(jax.ShapeDtypeStruct((t, ns), F32))
        out_specs.append(pl.BlockSpec((tm, ns), lambda i: (i, 0)))
        args.append(w_small)
    outs = pl.pallas_call(
        functools.partial(_inproj_kernel, gelu=gelu, has_small=w_small is not None, chunk=chunk),
        out_shape=out_shape,
        grid=(t // tm,),
        in_specs=in_specs,
        out_specs=out_specs,
        compiler_params=_cparams("parallel"),
        name="inproj",
    )(*args)
    return outs if w_small is not None else outs[0]


def _qkv_kernel(x_ref, g_ref, mod_ref, w_ref, *refs, rope, n_rot, q_cols, scale):
    if rope:
        cos_ref, sa_ref, sb_ref, o_ref = refs
    else:
        (o_ref,) = refs
    a = _norm_mod(x_ref[...], g_ref[...], mod_ref[0:1, :], mod_ref[1:2, :]).astype(BF16)
    y = jnp.dot(a, w_ref[...], preferred_element_type=F32)
    n = y.shape[1]
    if rope:
        cos, sa, sb = cos_ref[...], sa_ref[...], sb_ref[...]
    for s in range(n // LANES):
        ys = y[:, s * LANES:(s + 1) * LANES]
        if rope and s < n_rot:
            ys = ys * cos + pltpu.roll(ys, LANES - 16, 1) * sa + pltpu.roll(ys, 16, 1) * sb
        if s * LANES < q_cols:
            ys = ys * scale
        o_ref[:, s * LANES:(s + 1) * LANES] = ys.astype(o_ref.dtype)


def _qkv_proj(h, g, mod, w, tables, *, seq, tm, q_cols, k_cols):
    t, d = h.shape
    n = w.shape[1]
    tiles_per_seq = seq // tm
    bm = mod.shape[0]
    mod_idx = (lambda i: (i // tiles_per_seq, 0, 0)) if bm > 1 else (lambda i: (0, 0, 0))
    in_specs = [
        pl.BlockSpec((tm, d), lambda i: (i, 0)),
        pl.BlockSpec((1, d), lambda i: (0, 0)),
        pl.BlockSpec((None, 2, d), mod_idx),
        pl.BlockSpec((d, n), lambda i: (0, 0)),
    ]
    args = [h, g, mod, w]
    rope = tables is not None
    if rope:
        for tb in tables:
            in_specs.append(pl.BlockSpec((tm, LANES), lambda i: (i % tiles_per_seq, 0)))
            args.append(tb)
    return pl.pallas_call(
        functools.partial(_qkv_kernel, rope=rope, n_rot=(q_cols + k_cols) // LANES, q_cols=q_cols,
                          scale=ATTN_HEAD_DIM ** -0.5),
        out_shape=jax.ShapeDtypeStruct((t, n), BF16),
        grid=(t // tm,),
        in_specs=in_specs,
        out_specs=pl.BlockSpec((tm, n), lambda i: (i, 0)),
        compiler_params=_cparams("parallel"),
        name="qkv_proj",
    )(*args)


def _outproj_kernel(a_ref, w_ref, h_ref, mod_ref, o_ref):
    y = jnp.dot(a_ref[...], w_ref[...], preferred_element_type=F32)
    o_ref[...] = h_ref[...] + mod_ref[...] * y


def _outproj(a, w, h, gate, *, seq, tm):
    t, k = a.shape
    d = w.shape[1]
    tiles_per_seq = seq // tm
    bm = gate.shape[0]
    mod_idx = (lambda i: (i // tiles_per_seq, 0, 0)) if bm > 1 else (lambda i: (0, 0, 0))
    return pl.pallas_call(
        _outproj_kernel,
        out_shape=jax.ShapeDtypeStruct((t, d), F32),
        grid=(t // tm,),
        in_specs=[
            pl.BlockSpec((tm, k), lambda i: (i, 0)),
            pl.BlockSpec((k, d), lambda i: (0, 0)),
            pl.BlockSpec((tm, d), lambda i: (i, 0)),
            pl.BlockSpec((None, 1, d), mod_idx),
        ],
        out_specs=pl.BlockSpec((tm, d), lambda i: (i, 0)),
        compiler_params=_cparams("parallel"),
        name="outproj",
    )(a, w, h, gate)


def _ffn_kernel(x_ref, g_ref, mod_ref, win_ref, wo_ref, *refs, final, hidden, chunk):
    if final:
        fg_ref, o_ref = refs
    else:
        (o_ref,) = refs
    x = x_ref[...]
    a = _norm_mod(x, g_ref[...], mod_ref[0:1, :], mod_ref[1:2, :]).astype(BF16)
    dot = functools.partial(jnp.dot, preferred_element_type=F32)
    acc = None
    for c0 in range(0, hidden, chunk):
        gte = dot(a, win_ref[:, c0:c0 + chunk])
        up = dot(a, win_ref[:, hidden + c0:hidden + c0 + chunk])
        part = dot((_silu(gte) * up).astype(BF16), wo_ref[c0:c0 + chunk, :])
        acc = part if acc is None else acc + part
    y = x + mod_ref[2:3, :] * acc
    if final:
        y = y * lax.rsqrt(jnp.mean(y * y, axis=-1, keepdims=True) + RMS_EPS) * fg_ref[...]
    o_ref[...] = y


def _ffn(h, g, mod, w_in, w_out, *, seq, tm, chunk, final_g=None):
    t, d = h.shape
    hidden = w_out.shape[0]
    tiles_per_seq = seq // tm
    bm = mod.shape[0]
    mod_idx = (lambda i: (i // tiles_per_seq, 0, 0)) if bm > 1 else (lambda i: (0, 0, 0))
    resident = pl.Buffered(1)
    in_specs = [
        pl.BlockSpec((tm, d), lambda i: (i, 0)),
        pl.BlockSpec((1, d), lambda i: (0, 0)),
        pl.BlockSpec((None, 3, d), mod_idx),
        pl.BlockSpec((d, 2 * hidden), lambda i: (0, 0), pipeline_mode=resident),
        pl.BlockSpec((hidden, d), lambda i: (0, 0), pipeline_mode=resident),
    ]
    args = [h, g, mod, w_in, w_out]
    if final_g is not None:
        in_specs.append(pl.BlockSpec((1, d), lambda i: (0, 0)))
        args.append(final_g)
    return pl.pallas_call(
        functools.partial(_ffn_kernel, final=final_g is not None, hidden=hidden, chunk=chunk),
        out_shape=jax.ShapeDtypeStruct((t, d), F32),
        grid=(t // tm,),
        in_specs=in_specs,
        out_specs=pl.BlockSpec((tm, d), lambda i: (i, 0)),
        compiler_params=_cparams("parallel"),
        name="ffn",
    )(*args)


def _conv_kernel(main_ref, prev_ref, next_ref, w_ref, b_ref, o_ref, ext_scr, *, tiles_per_seq, tm):
    pos = pl.program_id(0) % tiles_per_seq
    half = SSD_CONV_W // 2
    ext_scr[0:SUBLANES, :] = jnp.where(pos == 0, 0.0, prev_ref[...])
    ext_scr[SUBLANES:SUBLANES + tm, :] = main_ref[...]
    ext_scr[SUBLANES + tm:, :] = jnp.where(pos == tiles_per_seq - 1, 0.0, next_ref[...])
    acc = jnp.broadcast_to(b_ref[...], (tm, b_ref.shape[1]))
    for k in range(SSD_CONV_W):
        acc = acc + w_ref[k:k + 1, :] * ext_scr[SUBLANES - half + k:SUBLANES - half + k + tm, :]
    o_ref[...] = _silu(acc)


def _conv_silu(zx, conv_w, conv_b, *, seq, tm, col0, tc):
    t = zx.shape[0]
    c = conv_w.shape[1]
    tiles_per_seq = seq // tm
    cb0 = col0 // tc
    rb = tm // SUBLANES
    last_rb = t // SUBLANES - 1
    return pl.pallas_call(
        functools.partial(_conv_kernel, tiles_per_seq=tiles_per_seq, tm=tm),
        out_shape=jax.ShapeDtypeStruct((t, c), F32),
        grid=(t // tm, c // tc),
        in_specs=[
            pl.BlockSpec((tm, tc), lambda i, j: (i, cb0 + j)),
            pl.BlockSpec((SUBLANES, tc), lambda i, j: (jnp.maximum(i * rb - 1, 0), cb0 + j)),
            pl.BlockSpec((SUBLANES, tc), lambda i, j: (jnp.minimum((i + 1) * rb, last_rb), cb0 + j)),
            pl.BlockSpec((SSD_CONV_W, tc), lambda i, j: (0, j)),
            pl.BlockSpec((1, tc), lambda i, j: (0, j)),
        ],
        out_specs=pl.BlockSpec((tm, tc), lambda i, j: (i, j)),
        scratch_shapes=[pltpu.VMEM((tm + 2 * SUBLANES, tc), F32)],
        compiler_params=_cparams("parallel", "parallel"),
        name="ssd_conv",
    )(zx, zx, zx, conv_w, conv_b)


def _ssd_kernel(x_ref, b_ref, c_ref, dt_ref, bias_ref, alog_ref, s0_ref, *refs,
                reverse, finish, heads, col_off):
    if finish:
        yf_ref, z_ref, d_ref, nw_ref, o_ref, sfin_ref, state, y_scr = refs
    else:
        o_ref, sfin_ref, state = refs
        y_scr = o_ref
    L = SSD_CHUNK
    hd = SSD_HEADDIM
    hpg = heads // SSD_GROUPS
    gw = hpg * hd
    pairs = gw // LANES
    j = pl.program_id(1)

    @pl.when(j == 0)
    def _():
        state[...] = s0_ref[...]

    row = lax.broadcasted_iota(jnp.int32, (L, L), 0)
    col = lax.broadcasted_iota(jnp.int32, (L, L), 1)
    mask = (col >= row) if reverse else (col <= row)
    tri = jnp.where(mask, 1.0, 0.0).astype(BF16)
    first = col < hd
    first_row = lax.broadcasted_iota(jnp.int32, (1, LANES), 1) < hd

    v = dt_ref[...] + bias_ref[...]
    dt = jnp.maximum(v, 0.0) + jnp.log1p(jnp.exp(-jnp.abs(v)))
    dta = dt * (-jnp.exp(alog_ref[...]))
    p1, p2, p3 = _split3(dta)
    dot = functools.partial(jnp.dot, preferred_element_type=F32)
    a_cum = dot(tri, p1) + (dot(tri, p2) + dot(tri, p3))
    b_log = a_cum - jnp.log(dt)
    a_tot = a_cum[0:1, :] if reverse else a_cum[L - 1:L, :]
    b_t = b_log.T
    dtw_t = jnp.exp(a_tot - b_log).T
    e_tot = jnp.exp(a_tot)

    for g in range(SSD_GROUPS):
        bg = b_ref[:, g * SSD_STATE:(g + 1) * SSD_STATE]
        cg = c_ref[:, g * SSD_STATE:(g + 1) * SSD_STATE].astype(BF16)
        cb = lax.dot_general(cg, bg.astype(BF16), (((1,), (1,)), ((), ())), preferred_element_type=F32)
        bg_t = bg.T
        y_in = dot(cg, state[:, g * gw:(g + 1) * gw].astype(BF16))
        for k in range(pairs):
            ca = col_off + g * hpg + 2 * k
            lo = g * gw + k * LANES
            xp = x_ref[:, lo:lo + LANES]
            x_a = jnp.where(first, xp, 0.0).astype(BF16)
            x_b = jnp.where(first, 0.0, xp).astype(BF16)
            a_a = jnp.broadcast_to(a_cum[:, ca:ca + 1], (L, L))
            a_b = jnp.broadcast_to(a_cum[:, ca + 1:ca + 2], (L, L))
            m_a = (cb * jnp.exp(jnp.where(mask, a_a - b_t[ca:ca + 1, :], -jnp.inf))).astype(BF16)
            m_b = (cb * jnp.exp(jnp.where(mask, a_b - b_t[ca + 1:ca + 2, :], -jnp.inf))).astype(BF16)
            e_cum = jnp.where(first, jnp.exp(a_a), jnp.exp(a_b))
            y_scr[:, lo:lo + LANES] = dot(m_a, x_a) + dot(m_b, x_b) + y_in[:, k * LANES:(k + 1) * LANES] * e_cum
            bt_a = (bg_t * dtw_t[ca:ca + 1, :]).astype(BF16)
            bt_b = (bg_t * dtw_t[ca + 1:ca + 2, :]).astype(BF16)
            e_end = jnp.where(first_row, e_tot[:, ca:ca + 1], e_tot[:, ca + 1:ca + 2])
            state[:, lo:lo + LANES] = state[:, lo:lo + LANES] * e_end + (dot(bt_a, x_a) + dot(bt_b, x_b))

    if finish:
        y = y_scr[...] + yf_ref[...] + x_ref[...] * d_ref[...]
        gated = y * _silu(z_ref[...])
        gn = gated * lax.rsqrt(jnp.mean(gated * gated, axis=-1, keepdims=True) + RMS_EPS) * nw_ref[...]
        o_ref[...] = gn.astype(o_ref.dtype)

    @pl.when(j == pl.num_programs(1) - 1)
    def _():
        sfin_ref[...] = state[...]


def _ssd_scan(xc, dt_raw, dt_bias, a_log, state0, *, batch, seq, reverse, d_inner, finish_args=None):
    t = xc.shape[0]
    L = SSD_CHUNK
    nc = seq // L
    heads = d_inner // SSD_HEADDIM
    gn = SSD_GROUPS * SSD_STATE
    finish = finish_args is not None

    def rows(b, j):
        return b * nc + ((nc - 1 - j) if reverse else j)

    in_specs = [
        pl.BlockSpec((L, d_inner), lambda b, j: (rows(b, j), 0)),
        pl.BlockSpec((L, gn), lambda b, j: (rows(b, j), d_inner // gn)),
        pl.BlockSpec((L, gn), lambda b, j: (rows(b, j), d_inner // gn + 1)),
        pl.BlockSpec((L, LANES), lambda b, j: (rows(b, j), 0)),
        pl.BlockSpec((1, LANES), lambda b, j: (0, 0)),
        pl.BlockSpec((1, LANES), lambda b, j: (0, 0)),
        pl.BlockSpec((None, SSD_STATE, d_inner), lambda b, j: (b, 0, 0)),
    ]
    args = [xc, xc, xc, dt_raw, dt_bias, a_log, state0]
    if finish:
        y_f, zx, d_exp, norm_w = finish_args
        in_specs += [
            pl.BlockSpec((L, d_inner), lambda b, j: (rows(b, j), 0)),
            pl.BlockSpec((L, d_inner), lambda b, j: (rows(b, j), 0)),
            pl.BlockSpec((1, d_inner), lambda b, j: (0, 0)),
            pl.BlockSpec((1, d_inner), lambda b, j: (0, 0)),
        ]
        args += [y_f, zx, d_exp, norm_w]
    out_dtype = BF16 if finish else F32
    return pl.pallas_call(
        functools.partial(_ssd_kernel, reverse=reverse, finish=finish, heads=heads,
                          col_off=heads if reverse else 0),
        out_shape=[jax.ShapeDtypeStruct((t, d_inner), out_dtype),
                   jax.ShapeDtypeStruct((batch, SSD_STATE, d_inner), F32)],
        grid=(batch, nc),
        in_specs=in_specs,
        out_specs=[pl.BlockSpec((L, d_inner), lambda b, j: (rows(b, j), 0)),
                   pl.BlockSpec((None, SSD_STATE, d_inner), lambda b, j: (b, 0, 0))],
        scratch_shapes=[pltpu.VMEM((SSD_STATE, d_inner), F32)]
        + ([pltpu.VMEM((L, d_inner), F32)] if finish else []),
        compiler_params=_cparams("parallel", "arbitrary"),
        name="ssd_scan_bwd" if reverse else "ssd_scan_fwd",
    )(*args)


def _attn_kernel(sink_ref, q_ref, kc_ref, vc_ref, *refs, band, q_heads):
    if band:
        kp_ref, k0_ref, kn_ref, vp_ref, v0_ref, vn_ref, o_ref = refs
    else:
        (o_ref,) = refs
    hd = ATTN_HEAD_DIM
    group = q_heads // ATTN_KV_HEADS
    blk = q_ref.shape[0]
    dot = functools.partial(jnp.dot, preferred_element_type=F32)
    dot_t = lambda a, b: lax.dot_general(a, b, (((1,), (1,)), ((), ())), preferred_element_type=F32)
    if band:
        j = pl.program_id(1)
        t_idx = lax.broadcasted_iota(jnp.int32, (blk, blk), 0)
        u_idx = lax.broadcasted_iota(jnp.int32, (blk, blk), 1)
        ok_prev = u_idx >= t_idx + jnp.where(j > 0, 0, 2 * blk)
        ok_next = u_idx + jnp.where(j < pl.num_programs(1) - 1, 0, 2 * blk) <= t_idx
    for h in range(q_heads):
        kv = h // group
        ksl = slice(kv * hd, (kv + 1) * hd)
        q = q_ref[:, h * hd:(h + 1) * hd]
        sink = sink_ref[h]
        s_tiles, v_tiles = [], []
        for c0 in range(0, kc_ref.shape[0], blk):
            s_tiles.append(dot_t(q, kc_ref[c0:c0 + blk, ksl]))
            v_tiles.append(vc_ref[c0:c0 + blk, ksl])
        if band:
            s_tiles += [jnp.where(ok_prev, dot_t(q, kp_ref[:, ksl]), -jnp.inf),
                        dot_t(q, k0_ref[:, ksl]),
                        jnp.where(ok_next, dot_t(q, kn_ref[:, ksl]), -jnp.inf)]
            v_tiles += [vp_ref[:, ksl], v0_ref[:, ksl], vn_ref[:, ksl]]
        m = jnp.maximum(jnp.max(functools.reduce(jnp.maximum, s_tiles), axis=-1, keepdims=True), sink)
        p_tiles = [jnp.exp(s - m) for s in s_tiles]
        den = jnp.exp(sink - m) + jnp.sum(functools.reduce(jnp.add, p_tiles), axis=-1, keepdims=True)
        o = functools.reduce(jnp.add, [dot(p.astype(BF16), v) for p, v in zip(p_tiles, v_tiles)])
        o_ref[:, h * hd:(h + 1) * hd] = (o * (1.0 / den)).astype(o_ref.dtype)


def _attention(sink, qkv, qkv_ctx, *, batch, seq, n_ctx, q_heads, band):
    blk = ATTN_BLOCK
    nb = seq // blk
    qd = q_heads * ATTN_HEAD_DIM
    kd = ATTN_KV_HEADS * ATTN_HEAD_DIM
    kcol, vcol = qd // kd, qd // kd + 1
    in_specs = [
        pl.BlockSpec(memory_space=pltpu.SMEM),
        pl.BlockSpec((blk, qd), lambda b, j: (b * nb + j, 0)),
        pl.BlockSpec((n_ctx, kd), lambda b, j: (b, kcol)),
        pl.BlockSpec((n_ctx, kd), lambda b, j: (b, vcol)),
    ]
    args = [sink, qkv, qkv_ctx, qkv_ctx]
    if band:
        prev = lambda b, j: b * nb + jnp.maximum(j - 1, 0)
        cur = lambda b, j: b * nb + j
        nxt = lambda b, j: b * nb + jnp.minimum(j + 1, nb - 1)
        for colb in (kcol, vcol):
            for f in (prev, cur, nxt):
                in_specs.append(pl.BlockSpec((blk, kd), functools.partial(lambda b, j, f, colb: (f(b, j), colb),
                                                                          f=f, colb=colb)))
                args.append(qkv)
    return pl.pallas_call(
        functools.partial(_attn_kernel, band=band, q_heads=q_heads),
        out_shape=jax.ShapeDtypeStruct((batch * seq, qd), BF16),
        grid=(batch, nb),
        in_specs=in_specs,
        out_specs=pl.BlockSpec((blk, qd), lambda b, j: (b * nb + j, 0)),
        compiler_params=_cparams("parallel", "parallel"),
        name="window_attn" if band else "ctx_attn",
    )(*args)


def _gmlp_kernel(u_ref, v_ref, lg_ref, lb_ref, ws_ref, bs_ref, o_ref):
    v = v_ref[...]
    mu = jnp.mean(v, axis=-1, keepdims=True)
    vc = v - mu
    var = jnp.mean(vc * vc, axis=-1, keepdims=True)
    vn = (vc * lax.rsqrt(var + LN_EPS) * lg_ref[...] + lb_ref[...]).astype(BF16)
    gd = v.shape[1] // GMLP_GROUPS
    for g in range(GMLP_GROUPS):
        sv = jnp.dot(ws_ref[g], vn[:, g * gd:(g + 1) * gd], preferred_element_type=F32) + bs_ref[:, g:g + 1]
        o_ref[:, g * gd:(g + 1) * gd] = (u_ref[:, g * gd:(g + 1) * gd] * sv).astype(o_ref.dtype)


def _gmlp_spatial(zz, ln_g, ln_b, w_s, b_s_t):
    t = zz.shape[0]
    width = zz.shape[1] // 2
    ch = GMLP_CHUNK
    return pl.pallas_call(
        _gmlp_kernel,
        out_shape=jax.ShapeDtypeStruct((t, width), BF16),
        grid=(t // ch,),
        in_specs=[
            pl.BlockSpec((ch, width), lambda i: (i, 0)),
            pl.BlockSpec((ch, width), lambda i: (i, 1)),
            pl.BlockSpec((1, width), lambda i: (0, 0)),
            pl.BlockSpec((1, width), lambda i: (0, 0)),
            pl.BlockSpec((GMLP_GROUPS, ch, ch), lambda i: (0, 0, 0)),
            pl.BlockSpec((ch, GMLP_GROUPS), lambda i: (0, 0)),
        ],
        out_specs=pl.BlockSpec((ch, width), lambda i: (i, 0)),
        compiler_params=_cparams("parallel"),
        name="gmlp_spatial",
    )(zz, zz, ln_g, ln_b, w_s, b_s_t)


def _rope_tables(n):
    freqs = ATTN_HEAD_DIM // 4
    rows = n // GRID_W
    row = jnp.repeat(jnp.arange(rows, dtype=jnp.int32), GRID_W, total_repeat_length=n)
    col = jnp.tile(jnp.arange(GRID_W, dtype=jnp.int32), rows)
    inv = ROPE_BASE ** (-jnp.arange(freqs, dtype=F32) / freqs)
    ang = jnp.stack([row.astype(F32)[:, None] * inv, col.astype(F32)[:, None] * inv], axis=1)
    ang = jnp.repeat(ang[:, :, None, :], 2, axis=2).reshape(n, ATTN_HEAD_DIM)
    cos, sin = jnp.cos(ang), jnp.sin(ang)
    reps = LANES // ATTN_HEAD_DIM
    cos, sin = jnp.tile(cos, (1, reps)), jnp.tile(sin, (1, reps))
    first_half = jnp.asarray((np.arange(LANES) // freqs) % 2 == 0)[None, :]
    return cos, jnp.where(first_half, -sin, 0.0), jnp.where(first_half, 0.0, sin)


def _ssd_mixer(streams, norm_g, w_in, conv_w, conv_b, a_log, dt_bias, d_skip, norm_w, w_out, need_ctx_out):
    d_inner = norm_w.shape[0]
    conv_dim = conv_w.shape[1]
    heads = d_inner // SSD_HEADDIM
    w_main = w_in[:, :d_inner + conv_dim].astype(BF16)
    w_dt = jnp.pad(w_in[:, d_inner + conv_dim:], ((0, 0), (0, LANES - 2 * heads))).astype(BF16)
    bias = jnp.pad(dt_bias.reshape(1, 2 * heads), ((0, 0), (0, LANES - 2 * heads)))
    alog = jnp.pad(a_log.reshape(1, 2 * heads), ((0, 0), (0, LANES - 2 * heads)))
    d_exp = jnp.repeat(d_skip, SSD_HEADDIM).reshape(1, d_inner)
    nw = norm_w.reshape(1, d_inner)
    w_out = w_out.astype(BF16)
    conv_b = conv_b.reshape(1, conv_dim)

    outs = []
    s_f = s_b = None
    for name, s in streams:
        if s_f is None:
            s_f = s_b = jnp.zeros((s["batch"], SSD_STATE, d_inner), F32)
        zx, dt_raw = _inproj(s["h"], norm_g, s["mod"][:, 0:2], w_main, seq=s["seq"], tm=s["tm"], chunk=1024,
                             w_small=w_dt)
        xc = _conv_silu(zx, conv_w, conv_b, seq=s["seq"], tm=s["tm"], col0=d_inner, tc=1024)
        y_f, s_f = _ssd_scan(xc, dt_raw, bias, alog, s_f, batch=s["batch"], seq=s["seq"], reverse=False,
                             d_inner=d_inner)
        gn, s_b = _ssd_scan(xc, dt_raw, bias, alog, s_b, batch=s["batch"], seq=s["seq"], reverse=True,
                            d_inner=d_inner, finish_args=(y_f, zx, d_exp, nw))
        if name == "ctx" and not need_ctx_out:
            outs.append(None)
        else:
            outs.append(_outproj(gn, w_out, s["h"], s["mod"][:, 2:3], seq=s["seq"], tm=s["tm"]))
    return outs


def _attn_mixer(streams, norm_g, w_qkv, sink, w_o, tables, need_ctx_out):
    (_, sc), (_, sl) = streams
    q_heads = sink.shape[0]
    qd = q_heads * ATTN_HEAD_DIM
    kd = ATTN_KV_HEADS * ATTN_HEAD_DIM
    w_qkv = w_qkv.astype(BF16)
    w_o = w_o.astype(BF16)
    qkv_c = _qkv_proj(sc["h"], norm_g, sc["mod"][:, 0:2], w_qkv, None, seq=sc["seq"], tm=sc["tm"],
                      q_cols=qd, k_cols=kd)
    qkv_l = _qkv_proj(sl["h"], norm_g, sl["mod"][:, 0:2], w_qkv, tables, seq=sl["seq"], tm=sl["tm"],
                      q_cols=qd, k_cols=kd)
    o_l = _attention(sink, qkv_l, qkv_c, batch=sl["batch"], seq=sl["seq"], n_ctx=sc["seq"], q_heads=q_heads,
                     band=True)
    h_l = _outproj(o_l, w_o, sl["h"], sl["mod"][:, 2:3], seq=sl["seq"], tm=sl["tm"])
    h_c = None
    if need_ctx_out:
        o_c = _attention(sink, qkv_c, qkv_c, batch=sc["batch"], seq=sc["seq"], n_ctx=sc["seq"], q_heads=q_heads,
                         band=False)
        h_c = _outproj(o_c, w_o, sc["h"], sc["mod"][:, 2:3], seq=sc["seq"], tm=sc["tm"])
    return [h_c, h_l]


def _gmlp_mixer(streams, norm_g, w_in, ln_g, ln_b, w_s, b_s, w_out, need_ctx_out):
    width = ln_g.shape[0]
    w_in = w_in.astype(BF16)
    w_out = w_out.astype(BF16)
    w_s = w_s.astype(BF16)
    outs = []
    for name, s in streams:
        if name == "ctx" and not need_ctx_out:
            outs.append(None)
            continue
        zz = _inproj(s["h"], norm_g, s["mod"][:, 0:2], w_in, seq=s["seq"], tm=s["tm"], chunk=1024, gelu=True)
        gm = _gmlp_spatial(zz, ln_g.reshape(1, width), ln_b.reshape(1, width), w_s, b_s.T)
        outs.append(_outproj(gm, w_out, s["h"], s["mod"][:, 2:3], seq=s["seq"], tm=s["tm"]))
    return outs


def kernel(x, c, ctx, c_ctx, w_mod, b_mod, norm_g, final_g, ssd_w_in, ssd_conv_w, ssd_conv_b, ssd_a_log, ssd_dt_bias, ssd_d, ssd_norm_w, ssd_w_out, attn_w_qkv, attn_sink, attn_w_o, gmlp_w_in, gmlp_ln_g, gmlp_ln_b, gmlp_w_s, gmlp_b_s, gmlp_w_out, ffn_w_in, ffn_w_out):
    batch, seq, d = x.shape
    n_ctx = ctx.shape[1]
    depth = w_mod.shape[0]
    hidden = ffn_w_out.shape[1]

    pad_rows = -(batch + 1) % 16
    cc = jnp.concatenate([c, c_ctx[None, :], jnp.zeros((pad_rows, d), F32)], axis=0)
    mods = _modulation(cc, w_mod, b_mod)
    tables = _rope_tables(seq)

    h_lat = x.reshape(batch * seq, d)
    h_ctx = ctx.reshape(batch * n_ctx, d)
    tm_lat = 512 if seq % 512 == 0 else 256
    tm_ctx = 256
    ffn_chunk = hidden

    for i in range(depth):
        last = i == depth - 1
        kind, j = i % 3, i // 3
        mod_lat = mods[i, :batch].reshape(batch, N_MOD, d)
        mod_ctx = mods[i, batch:batch + 1].reshape(1, N_MOD, d)
        g1 = norm_g[i, 0].reshape(1, d)
        g2 = norm_g[i, 1].reshape(1, d)
        streams = [
            ("ctx", dict(h=h_ctx, mod=mod_ctx, batch=batch, seq=n_ctx, tm=tm_ctx)),
            ("lat", dict(h=h_lat, mod=mod_lat, batch=batch, seq=seq, tm=tm_lat)),
        ]
        need_ctx_out = not last
        if kind == 0:
            h_ctx_new, h_lat = _ssd_mixer(streams, g1, ssd_w_in[j], ssd_conv_w[j], ssd_conv_b[j], ssd_a_log[j],
                                          ssd_dt_bias[j], ssd_d[j], ssd_norm_w[j], ssd_w_out[j], need_ctx_out)
        elif kind == 1:
            h_ctx_new, h_lat = _attn_mixer(streams, g1, attn_w_qkv[j], attn_sink[j], attn_w_o[j], tables,
                                           need_ctx_out)
        else:
            h_ctx_new, h_lat = _gmlp_mixer(streams, g1, gmlp_w_in[j], gmlp_ln_g[j], gmlp_ln_b[j], gmlp_w_s[j],
                                           gmlp_b_s[j], gmlp_w_out[j], need_ctx_out)

        w_in = ffn_w_in[i].astype(BF16)
        w_out = ffn_w_out[i].astype(BF16)
        h_lat = _ffn(h_lat, g2, mod_lat[:, 3:6], w_in, w_out, seq=seq, tm=tm_lat, chunk=ffn_chunk,
                     final_g=final_g.reshape(1, d) if last else None)
        if need_ctx_out:
            h_ctx = _ffn(h_ctx_new, g2, mod_ctx[:, 3:6], w_in, w_out, seq=n_ctx, tm=tm_ctx, chunk=ffn_chunk)
    return h_lat.reshape(batch, seq, d)
```

```python
import functools
import math

import jax
import jax.numpy as jnp
import numpy as np
from jax import lax
from jax.experimental import pallas as pl
from jax.experimental.pallas import tpu as pltpu

F32 = jnp.float32
BF16 = jnp.bfloat16

N_MOD = 6
GRID_W = 64
SSD_HEADDIM = 64
SSD_GROUPS = 4
SSD_STATE = 128
SSD_CONV_W = 5
SSD_CHUNK = 128
ATTN_HEAD_DIM = 64
ATTN_KV_HEADS = 4
ATTN_WINDOW = 128
ATTN_BLOCK = 128
ROPE_BASE = 10000.0
GMLP_GROUPS = 8
GMLP_CHUNK = 128
RMS_EPS = 1e-6
LN_EPS = 1e-5

LANES = 128
SUBLANES = 8
VMEM_LIMIT = 56 * 1024 * 1024


def _cparams(*sem):
    return pltpu.CompilerParams(dimension_semantics=sem, vmem_limit_bytes=VMEM_LIMIT)


def _silu(x):
    return x * (1.0 / (1.0 + jnp.exp(-x)))


def _norm_mod(x, g, shift, scale):
    y = x * lax.rsqrt(jnp.mean(x * x, axis=-1, keepdims=True) + RMS_EPS)
    return (y * g) * (1.0 + scale) + shift


def _split3(x):
    hi = x.astype(BF16)
    r1 = x - hi.astype(F32)
    mid = r1.astype(BF16)
    lo = (r1 - mid.astype(F32)).astype(BF16)
    return hi, mid, lo


def _mod_kernel(c_ref, w_ref, b_ref, o_ref):
    x = _silu(c_ref[...])
    w = w_ref[...]
    xh = x.astype(BF16)
    xl = (x - xh.astype(F32)).astype(BF16)
    wh = w.astype(BF16)
    wl = (w - wh.astype(F32)).astype(BF16)
    dot = functools.partial(jnp.dot, preferred_element_type=F32)
    o_ref[...] = dot(xh, wh) + (dot(xh, wl) + dot(xl, wh)) + b_ref[...]


def _modulation(cc, w_mod, b_mod):
    depth, d, n = w_mod.shape
    rows = cc.shape[0]
    tn = 1536
    return pl.pallas_call(
        _mod_kernel,
        out_shape=jax.ShapeDtypeStruct((depth, rows, n), F32),
        grid=(depth, n // tn),
        in_specs=[
            pl.BlockSpec((rows, d), lambda l, j: (0, 0)),
            pl.BlockSpec((None, d, tn), lambda l, j: (l, 0, j)),
            pl.BlockSpec((None, 1, tn), lambda l, j: (l, 0, j)),
        ],
        out_specs=pl.BlockSpec((None, rows, tn), lambda l, j: (l, 0, j)),
        compiler_params=_cparams("parallel", "parallel"),
        name="modulation",
    )(cc, w_mod, b_mod.reshape(depth, 1, n))


def _qkv_kernel(x_ref, g_ref, mod_ref, w_ref, *refs, rope, n_rot, q_cols, scale):
    if rope:
        cos_ref, sa_ref, sb_ref, o_ref = refs
    else:
        (o_ref,) = refs
    a = _norm_mod(x_ref[...], g_ref[...], mod_ref[0:1, :], mod_ref[1:2, :]).astype(BF16)
    y = jnp.dot(a, w_ref[...], preferred_element_type=F32)
    n = y.shape[1]
    if rope:
        cos, sa, sb = cos_ref[...], sa_ref[...], sb_ref[...]
    for s in range(n // LANES):
        ys = y[:, s * LANES:(s + 1) * LANES]
        if rope and s < n_rot:
            ys = ys * cos + pltpu.roll(ys, LANES - 16, 1) * sa + pltpu.roll(ys, 16, 1) * sb
        if s * LANES < q_cols:
            ys = ys * scale
        o_ref[:, s * LANES:(s + 1) * LANES] = ys.astype(o_ref.dtype)


def _qkv_proj(h, g, mod, w, tables, *, seq, tm, q_cols, k_cols):
    t, d = h.shape
    n = w.shape[1]
    tiles_per_seq = seq // tm
    bm = mod.shape[0]
    mod_idx = (lambda i: (i // tiles_per_seq, 0, 0)) if bm > 1 else (lambda i: (0, 0, 0))
    in_specs = [
        pl.BlockSpec((tm, d), lambda i: (i, 0)),
        pl.BlockSpec((1, d), lambda i: (0, 0)),
        pl.BlockSpec((None, 2, d), mod_idx),
        pl.BlockSpec((d, n), lambda i: (0, 0)),
    ]
    args = [h, g, mod, w]
    rope = tables is not None
    if rope:
        for tb in tables:
            in_specs.append(pl.BlockSpec((tm, LANES), lambda i: (i % tiles_per_seq, 0)))
            args.append(tb)
    return pl.pallas_call(
        functools.partial(_qkv_kernel, rope=rope, n_rot=(q_cols + k_cols) // LANES, q_cols=q_cols,
                          scale=ATTN_HEAD_DIM ** -0.5),
        out_shape=jax.ShapeDtypeStruct((t, n), BF16),
        grid=(t // tm,),
        in_specs=in_specs,
        out_specs=pl.BlockSpec((tm, n), lambda i: (i, 0)),
        compiler_params=_cparams("parallel"),
        name="qkv_proj",
    )(*args)


def _outproj_kernel(a_ref, w_ref, h_ref, mod_ref, o_ref):
    y = jnp.dot(a_ref[...], w_ref[...], preferred_element_type=F32)
    o_ref[...] = h_ref[...] + mod_ref[...] * y


def _outproj(a, w, h, gate, *, seq, tm):
    t, k = a.shape
    d = w.shape[1]
    tiles_per_seq = seq // tm
    bm = gate.shape[0]
    mod_idx = (lambda i: (i // tiles_per_seq, 0, 0)) if bm > 1 else (lambda i: (0, 0, 0))
    return pl.pallas_call(
        _outproj_kernel,
        out_shape=jax.ShapeDtypeStruct((t, d), F32),
        grid=(t // tm,),
        in_specs=[
            pl.BlockSpec((tm, k), lambda i: (i, 0)),
            pl.BlockSpec((k, d), lambda i: (0, 0)),
            pl.BlockSpec((tm, d), lambda i: (i, 0)),
            pl.BlockSpec((None, 1, d), mod_idx),
        ],
        out_specs=pl.BlockSpec((tm, d), lambda i: (i, 0)),
        compiler_params=_cparams("parallel"),
        name="outproj",
    )(a, w, h, gate)


def _ffn_kernel(x_ref, g_ref, mod_ref, win_ref, wo_ref, *refs, final, hidden, chunk):
    if final:
        fg_ref, o_ref = refs
    else:
        (o_ref,) = refs
    x = x_ref[...]
    a = _norm_mod(x, g_ref[...], mod_ref[0:1, :], mod_ref[1:2, :]).astype(BF16)
    dot = functools.partial(jnp.dot, preferred_element_type=F32)
    acc = None
    for c0 in range(0, hidden, chunk):
        gte = dot(a, win_ref[:, c0:c0 + chunk])
        up = dot(a, win_ref[:, hidden + c0:hidden + c0 + chunk])
        part = dot((_silu(gte) * up).astype(BF16), wo_ref[c0:c0 + chunk, :])
        acc = part if acc is None else acc + part
    y = x + mod_ref[2:3, :] * acc
    if final:
        y = y * lax.rsqrt(jnp.mean(y * y, axis=-1, keepdims=True) + RMS_EPS) * fg_ref[...]
    o_ref[...] = y


def _ffn(h, g, mod, w_in, w_out, *, seq, tm, chunk, final_g=None):
    t, d = h.shape
    hidden = w_out.shape[0]
    tiles_per_seq = seq // tm
    bm = mod.shape[0]
    mod_idx = (lambda i: (i // tiles_per_seq, 0, 0)) if bm > 1 else (lambda i: (0, 0, 0))
    resident = pl.Buffered(1)
    in_specs = [
        pl.BlockSpec((tm, d), lambda i: (i, 0)),
        pl.BlockSpec((1, d), lambda i: (0, 0)),
        pl.BlockSpec((None, 3, d), mod_idx),
        pl.BlockSpec((d, 2 * hidden), lambda i: (0, 0), pipeline_mode=resident),
        pl.BlockSpec((hidden, d), lambda i: (0, 0), pipeline_mode=resident),
    ]
    args = [h, g, mod, w_in, w_out]
    if final_g is not None:
        in_specs.append(pl.BlockSpec((1, d), lambda i: (0, 0)))
        args.append(final_g)
    return pl.pallas_call(
        functools.partial(_ffn_kernel, final=final_g is not None, hidden=hidden, chunk=chunk),
        out_shape=jax.ShapeDtypeStruct((t, d), F32),
        grid=(t // tm,),
        in_specs=in_specs,
        out_specs=pl.BlockSpec((tm, d), lambda i: (i, 0)),
        compiler_params=_cparams("parallel"),
        name="ffn",
    )(*args)


HALO = 16


def _ssd_inproj_kernel(x_ref, xp_ref, xn_ref, g_ref, mod_ref, w_ref, wdt_ref, cw_ref, cb_ref,
                       z_ref, xc_ref, dt_ref, a_scr, *, tiles_per_seq, tm, d_inner, chunk):
    pos = pl.program_id(0) % tiles_per_seq
    g, shift, scale = g_ref[...], mod_ref[0:1, :], mod_ref[1:2, :]
    a_prev = jnp.where(pos == 0, 0.0, _norm_mod(xp_ref[...], g, shift, scale))
    a_next = jnp.where(pos == tiles_per_seq - 1, 0.0, _norm_mod(xn_ref[...], g, shift, scale))
    a_scr[0:HALO, :] = a_prev.astype(BF16)
    a_scr[HALO:HALO + tm, :] = _norm_mod(x_ref[...], g, shift, scale).astype(BF16)
    a_scr[HALO + tm:, :] = a_next.astype(BF16)
    dot = functools.partial(jnp.dot, preferred_element_type=F32)
    a_main = a_scr[HALO:HALO + tm, :]
    dt_ref[...] = dot(a_main, wdt_ref[...])
    for c0 in range(0, d_inner, chunk):
        z_ref[:, c0:c0 + chunk] = dot(a_main, w_ref[:, c0:c0 + chunk])

    rows = tm + 2 * HALO
    half = SSD_CONV_W // 2
    spc = chunk // LANES
    for c in range(cw_ref.shape[0] // spc):
        pre = dot(a_scr[...], w_ref[:, d_inner + c * chunk:d_inner + (c + 1) * chunk])
        for s in range(spc):
            e = pre[:, s * LANES:(s + 1) * LANES]
            cw = cw_ref[c * spc + s]
            acc = cb_ref[c * spc + s] + cw[half:half + 1, :] * e[HALO:HALO + tm, :]
            for k in range(SSD_CONV_W):
                if k != half:
                    acc = acc + cw[k:k + 1, :] * pltpu.roll(e, (half - k) % rows, 0)[HALO:HALO + tm, :]
            xc_ref[c * spc + s] = _silu(acc)


def _ssd_inproj(h, g, mod, w_main, w_dt, conv_w, conv_b, *, seq, tm, d_inner, chunk):
    t, d = h.shape
    slabs = conv_w.shape[0]
    tiles_per_seq = seq // tm
    bm = mod.shape[0]
    mod_idx = (lambda i: (i // tiles_per_seq, 0, 0)) if bm > 1 else (lambda i: (0, 0, 0))
    rb = tm // HALO
    last_rb = t // HALO - 1
    resident = pl.Buffered(1)
    return pl.pallas_call(
        functools.partial(_ssd_inproj_kernel, tiles_per_seq=tiles_per_seq, tm=tm, d_inner=d_inner, chunk=chunk),
        out_shape=[jax.ShapeDtypeStruct((t, d_inner), F32),
                   jax.ShapeDtypeStruct((slabs, t, LANES), F32),
                   jax.ShapeDtypeStruct((t, w_dt.shape[1]), F32)],
        grid=(t // tm,),
        in_specs=[
            pl.BlockSpec((tm, d), lambda i: (i, 0)),
            pl.BlockSpec((HALO, d), lambda i: (jnp.maximum(i * rb - 1, 0), 0)),
            pl.BlockSpec((HALO, d), lambda i: (jnp.minimum((i + 1) * rb, last_rb), 0)),
            pl.BlockSpec((1, d), lambda i: (0, 0)),
            pl.BlockSpec((None, 2, d), mod_idx),
            pl.BlockSpec(w_main.shape, lambda i: (0, 0), pipeline_mode=resident),
            pl.BlockSpec(w_dt.shape, lambda i: (0, 0), pipeline_mode=resident),
            pl.BlockSpec(conv_w.shape, lambda i: (0, 0, 0)),
            pl.BlockSpec(conv_b.shape, lambda i: (0, 0, 0)),
        ],
        out_specs=[pl.BlockSpec((tm, d_inner), lambda i: (i, 0)),
                   pl.BlockSpec((slabs, tm, LANES), lambda i: (0, i, 0)),
                   pl.BlockSpec((tm, w_dt.shape[1]), lambda i: (i, 0))],
        scratch_shapes=[pltpu.VMEM((tm + 2 * HALO, d), BF16)],
        compiler_params=_cparams("parallel"),
        name="ssd_inproj",
    )(h, h, h, g, mod, w_main, w_dt, conv_w, conv_b)


def _ssd_kernel(x_ref, b_ref, c_ref, dt_ref, bias_ref, alog_ref, s0_ref, *refs,
                reverse, finish, heads, col_off):
    if finish:
        yf_ref, z_ref, d_ref, nw_ref, o_ref, sfin_ref, state, y_scr = refs
    else:
        o_ref, sfin_ref, state = refs
        y_scr = o_ref
    L = SSD_CHUNK
    hd = SSD_HEADDIM
    hpg = heads // SSD_GROUPS
    gw = hpg * hd
    pairs = gw // LANES
    j = pl.program_id(1)

    @pl.when(j == 0)
    def _():
        state[...] = s0_ref[...]

    row = lax.broadcasted_iota(jnp.int32, (L, L), 0)
    col = lax.broadcasted_iota(jnp.int32, (L, L), 1)
    mask = (col >= row) if reverse else (col <= row)
    tri = jnp.where(mask, 1.0, 0.0).astype(BF16)
    first = col < hd
    first_row = lax.broadcasted_iota(jnp.int32, (1, LANES), 1) < hd

    v = dt_ref[...] + bias_ref[...]
    dt = jnp.maximum(v, 0.0) + jnp.log1p(jnp.exp(-jnp.abs(v)))
    dta = dt * (-jnp.exp(alog_ref[...]))
    p1, p2, p3 = _split3(dta)
    dot = functools.partial(jnp.dot, preferred_element_type=F32)
    a_cum = dot(tri, p1) + (dot(tri, p2) + dot(tri, p3))
    b_log = a_cum - jnp.log(dt)
    a_tot = a_cum[0:1, :] if reverse else a_cum[L - 1:L, :]
    b_t = b_log.T
    dtw_t = jnp.exp(a_tot - b_log).T
    e_tot = jnp.exp(a_tot)

    for g in range(SSD_GROUPS):
        bg = b_ref[g]
        cg = c_ref[g].astype(BF16)
        cb = lax.dot_general(cg, bg.astype(BF16), (((1,), (1,)), ((), ())), preferred_element_type=F32)
        bg_t = bg.T
        y_in = dot(cg, state[:, g * gw:(g + 1) * gw].astype(BF16))
        for k in range(pairs):
            ca = col_off + g * hpg + 2 * k
            lo = g * gw + k * LANES
            xp = x_ref[g * pairs + k]
            x_a = jnp.where(first, xp, 0.0).astype(BF16)
            x_b = jnp.where(first, 0.0, xp).astype(BF16)
            a_a = jnp.broadcast_to(a_cum[:, ca:ca + 1], (L, L))
            a_b = jnp.broadcast_to(a_cum[:, ca + 1:ca + 2], (L, L))
            m_a = (cb * jnp.exp(jnp.where(mask, a_a - b_t[ca:ca + 1, :], -jnp.inf))).astype(BF16)
            m_b = (cb * jnp.exp(jnp.where(mask, a_b - b_t[ca + 1:ca + 2, :], -jnp.inf))).astype(BF16)
            e_cum = jnp.where(first, jnp.exp(a_a), jnp.exp(a_b))
            y_pair = dot(m_a, x_a) + dot(m_b, x_b) + y_in[:, k * LANES:(k + 1) * LANES] * e_cum
            if finish:
                y_pair = y_pair + xp * d_ref[:, lo:lo + LANES]
            y_scr[:, lo:lo + LANES] = y_pair
            bt_a = (bg_t * dtw_t[ca:ca + 1, :]).astype(BF16)
            bt_b = (bg_t * dtw_t[ca + 1:ca + 2, :]).astype(BF16)
            e_end = jnp.where(first_row, e_tot[:, ca:ca + 1], e_tot[:, ca + 1:ca + 2])
            state[:, lo:lo + LANES] = state[:, lo:lo + LANES] * e_end + (dot(bt_a, x_a) + dot(bt_b, x_b))

    if finish:
        y = y_scr[...] + yf_ref[...]
        gated = y * _silu(z_ref[...])
        gn = gated * lax.rsqrt(jnp.mean(gated * gated, axis=-1, keepdims=True) + RMS_EPS) * nw_ref[...]
        o_ref[...] = gn.astype(o_ref.dtype)

    @pl.when(j == pl.num_programs(1) - 1)
    def _():
        sfin_ref[...] = state[...]


def _ssd_scan(xc, dt_raw, dt_bias, a_log, state0, *, batch, seq, reverse, d_inner, finish_args=None):
    t = xc.shape[1]
    L = SSD_CHUNK
    nc = seq // L
    heads = d_inner // SSD_HEADDIM
    x_slabs = d_inner // LANES
    gs = SSD_GROUPS * SSD_STATE // LANES
    assert SSD_STATE == LANES and x_slabs % gs == 0
    finish = finish_args is not None

    def rows(b, j):
        return b * nc + ((nc - 1 - j) if reverse else j)

    in_specs = [
        pl.BlockSpec((x_slabs, L, LANES), lambda b, j: (0, rows(b, j), 0)),
        pl.BlockSpec((gs, L, LANES), lambda b, j: (x_slabs // gs, rows(b, j), 0)),
        pl.BlockSpec((gs, L, LANES), lambda b, j: (x_slabs // gs + 1, rows(b, j), 0)),
        pl.BlockSpec((L, LANES), lambda b, j: (rows(b, j), 0)),
        pl.BlockSpec((1, LANES), lambda b, j: (0, 0)),
        pl.BlockSpec((1, LANES), lambda b, j: (0, 0)),
        pl.BlockSpec((None, SSD_STATE, d_inner), lambda b, j: (b, 0, 0)),
    ]
    args = [xc, xc, xc, dt_raw, dt_bias, a_log, state0]
    if finish:
        y_f, z, d_exp, norm_w = finish_args
        in_specs += [
            pl.BlockSpec((L, d_inner), lambda b, j: (rows(b, j), 0)),
            pl.BlockSpec((L, d_inner), lambda b, j: (rows(b, j), 0)),
            pl.BlockSpec((1, d_inner), lambda b, j: (0, 0)),
            pl.BlockSpec((1, d_inner), lambda b, j: (0, 0)),
        ]
        args += [y_f, z, d_exp, norm_w]
    out_dtype = BF16 if finish else F32
    return pl.pallas_call(
        functools.partial(_ssd_kernel, reverse=reverse, finish=finish, heads=heads,
                          col_off=heads if reverse else 0),
        out_shape=[jax.ShapeDtypeStruct((t, d_inner), out_dtype),
                   jax.ShapeDtypeStruct((batch, SSD_STATE, d_inner), F32)],
        grid=(batch, nc),
        in_specs=in_specs,
        out_specs=[pl.BlockSpec((L, d_inner), lambda b, j: (rows(b, j), 0)),
                   pl.BlockSpec((None, SSD_STATE, d_inner), lambda b, j: (b, 0, 0))],
        scratch_shapes=[pltpu.VMEM((SSD_STATE, d_inner), F32)]
        + ([pltpu.VMEM((L, d_inner), F32)] if finish else []),
        compiler_params=_cparams("parallel", "arbitrary"),
        name="ssd_scan_bwd" if reverse else "ssd_scan_fwd",
    )(*args)


def _attn_kernel(sink_ref, q_ref, kc_ref, vc_ref, *refs, band, q_heads):
    if band:
        kp_ref, k0_ref, kn_ref, vp_ref, v0_ref, vn_ref, o_ref = refs
    else:
        (o_ref,) = refs
    hd = ATTN_HEAD_DIM
    group = q_heads // ATTN_KV_HEADS
    blk = q_ref.shape[0]
    dot = functools.partial(jnp.dot, preferred_element_type=F32)
    dot_t = lambda a, b: lax.dot_general(a, b, (((1,), (1,)), ((), ())), preferred_element_type=F32)
    gq = group * blk
    key_idx = lax.broadcasted_iota(jnp.int32, (blk, gq), 0)
    qry_idx = lax.broadcasted_iota(jnp.int32, (blk, gq), 1) % blk
    lane_head = lax.broadcasted_iota(jnp.int32, (1, gq), 1) // blk
    segs = [(kc_ref[c0:c0 + blk, :], vc_ref[c0:c0 + blk, :], None) for c0 in range(0, kc_ref.shape[0], blk)]
    if band:
        j = pl.program_id(1)
        ok_prev = key_idx >= qry_idx + jnp.where(j > 0, 0, 2 * blk)
        ok_next = key_idx + jnp.where(j < pl.num_programs(1) - 1, 0, 2 * blk) <= qry_idx
        segs += [(kp_ref[...], vp_ref[...], ok_prev), (k0_ref[...], v0_ref[...], None),
                 (kn_ref[...], vn_ref[...], ok_next)]
    v_t = [v.astype(F32).T.astype(BF16) for _, v, _ in segs]
    o_rows = []
    for kv in range(ATTN_KV_HEADS):
        ksl = slice(kv * hd, (kv + 1) * hd)
        h0 = kv * group
        qs = jnp.concatenate([q_ref[:, (h0 + i) * hd:(h0 + i + 1) * hd] for i in range(group)], axis=0)
        sink = jnp.full((1, gq), sink_ref[h0], F32)
        for i in range(1, group):
            sink = jnp.where(lane_head == i, sink_ref[h0 + i], sink)
        s_tiles = []
        for k, _, ok in segs:
            s = dot_t(k[:, ksl], qs)
            s_tiles.append(s if ok is None else jnp.where(ok, s, -jnp.inf))
        m = jnp.maximum(jnp.max(functools.reduce(jnp.maximum, s_tiles), axis=0, keepdims=True), sink)
        p_tiles = [jnp.exp(s - m) for s in s_tiles]
        den = jnp.exp(sink - m) + jnp.sum(functools.reduce(jnp.add, p_tiles), axis=0, keepdims=True)
        o_t = functools.reduce(jnp.add, [dot(vt[ksl, :], p.astype(BF16)) for vt, p in zip(v_t, p_tiles)])
        o_t = o_t * (1.0 / den)
        o_rows += [o_t[:, i * blk:(i + 1) * blk] for i in range(group)]
    o_ref[...] = jnp.concatenate(o_rows, axis=0).T.astype(o_ref.dtype)


def _attention(sink, qkv, qkv_ctx, *, batch, seq, n_ctx, q_heads, band):
    blk = ATTN_BLOCK
    nb = seq // blk
    qd = q_heads * ATTN_HEAD_DIM
    kd = ATTN_KV_HEADS * ATTN_HEAD_DIM
    kcol, vcol = qd // kd, qd // kd + 1
    in_specs = [
        pl.BlockSpec(memory_space=pltpu.SMEM),
        pl.BlockSpec((blk, qd), lambda b, j: (b * nb + j, 0)),
        pl.BlockSpec((n_ctx, kd), lambda b, j: (b, kcol)),
        pl.BlockSpec((n_ctx, kd), lambda b, j: (b, vcol)),
    ]
    args = [sink, qkv, qkv_ctx, qkv_ctx]
    if band:
        prev = lambda b, j: b * nb + jnp.maximum(j - 1, 0)
        cur = lambda b, j: b * nb + j
        nxt = lambda b, j: b * nb + jnp.minimum(j + 1, nb - 1)
        for colb in (kcol, vcol):
            for f in (prev, cur, nxt):
                in_specs.append(pl.BlockSpec((blk, kd), functools.partial(lambda b, j, f, colb: (f(b, j), colb),
                                                                          f=f, colb=colb)))
                args.append(qkv)
    return pl.pallas_call(
        functools.partial(_attn_kernel, band=band, q_heads=q_heads),
        out_shape=jax.ShapeDtypeStruct((batch * seq, qd), BF16),
        grid=(batch, nb),
        in_specs=in_specs,
        out_specs=pl.BlockSpec((blk, qd), lambda b, j: (b * nb + j, 0)),
        compiler_params=_cparams("parallel", "parallel"),
        name="window_attn" if band else "ctx_attn",
    )(*args)


def _gmlp_kernel(x_ref, g_ref, mod_ref, win_ref, lg_ref, lb_ref, ws_ref, bs_ref, wo_ref, o_ref, gm_scr, *, chunk):
    x = x_ref[...]
    a = _norm_mod(x, g_ref[...], mod_ref[0:1, :], mod_ref[1:2, :]).astype(BF16)
    width = wo_ref.shape[0]
    dot = functools.partial(jnp.dot, preferred_element_type=F32)

    def gelu_proj(c0):
        y = dot(a, win_ref[:, c0:c0 + chunk])
        return 0.5 * y * (1.0 + lax.erf(y * (1.0 / math.sqrt(2.0))))

    v = jnp.concatenate([gelu_proj(width + c0) for c0 in range(0, width, chunk)], axis=1)
    mu = jnp.mean(v, axis=-1, keepdims=True)
    vc = v - mu
    var = jnp.mean(vc * vc, axis=-1, keepdims=True)
    vn = (vc * lax.rsqrt(var + LN_EPS) * lg_ref[...] + lb_ref[...]).astype(BF16)
    gd = width // GMLP_GROUPS
    ch = GMLP_CHUNK
    for c0 in range(0, width, chunk):
        u = gelu_proj(c0)
        for g in range(c0 // gd, (c0 + chunk) // gd):
            for r in range(0, x.shape[0], ch):
                sv = dot(ws_ref[g], vn[r:r + ch, g * gd:(g + 1) * gd]) + bs_ref[:, g:g + 1]
                gm_scr[r:r + ch, g * gd:(g + 1) * gd] = (u[r:r + ch, g * gd - c0:(g + 1) * gd - c0] * sv).astype(BF16)
    o_ref[...] = x + mod_ref[2:3, :] * dot(gm_scr[...], wo_ref[...])


def _gmlp(h, g, mod, w_in, ln_g, ln_b, w_s, b_s_t, w_out, *, seq, tm, chunk):
    t, d = h.shape
    width = w_out.shape[0]
    assert tm % GMLP_CHUNK == 0 and chunk % (width // GMLP_GROUPS) == 0
    tiles_per_seq = seq // tm
    bm = mod.shape[0]
    mod_idx = (lambda i: (i // tiles_per_seq, 0, 0)) if bm > 1 else (lambda i: (0, 0, 0))
    resident = pl.Buffered(1)
    return pl.pallas_call(
        functools.partial(_gmlp_kernel, chunk=chunk),
        out_shape=jax.ShapeDtypeStruct((t, d), F32),
        grid=(t // tm,),
        in_specs=[
            pl.BlockSpec((tm, d), lambda i: (i, 0)),
            pl.BlockSpec((1, d), lambda i: (0, 0)),
            pl.BlockSpec((None, 3, d), mod_idx),
            pl.BlockSpec(w_in.shape, lambda i: (0, 0), pipeline_mode=resident),
            pl.BlockSpec((1, width), lambda i: (0, 0)),
            pl.BlockSpec((1, width), lambda i: (0, 0)),
            pl.BlockSpec(w_s.shape, lambda i: (0, 0, 0)),
            pl.BlockSpec(b_s_t.shape, lambda i: (0, 0)),
            pl.BlockSpec(w_out.shape, lambda i: (0, 0), pipeline_mode=resident),
        ],
        out_specs=pl.BlockSpec((tm, d), lambda i: (i, 0)),
        scratch_shapes=[pltpu.VMEM((tm, width), BF16)],
        compiler_params=_cparams("parallel"),
        name="gmlp",
    )(h, g, mod, w_in, ln_g, ln_b, w_s, b_s_t, w_out)


def _rope_tables(n):
    freqs = ATTN_HEAD_DIM // 4
    rows = n // GRID_W
    row = jnp.repeat(jnp.arange(rows, dtype=jnp.int32), GRID_W, total_repeat_length=n)
    col = jnp.tile(jnp.arange(GRID_W, dtype=jnp.int32), rows)
    inv = ROPE_BASE ** (-jnp.arange(freqs, dtype=F32) / freqs)
    ang = jnp.stack([row.astype(F32)[:, None] * inv, col.astype(F32)[:, None] * inv], axis=1)
    ang = jnp.repeat(ang[:, :, None, :], 2, axis=2).reshape(n, ATTN_HEAD_DIM)
    cos, sin = jnp.cos(ang), jnp.sin(ang)
    reps = LANES // ATTN_HEAD_DIM
    cos, sin = jnp.tile(cos, (1, reps)), jnp.tile(sin, (1, reps))
    first_half = jnp.asarray((np.arange(LANES) // freqs) % 2 == 0)[None, :]
    return cos, jnp.where(first_half, -sin, 0.0), jnp.where(first_half, 0.0, sin)


def _ssd_mixer(streams, norm_g, w_in, conv_w, conv_b, a_log, dt_bias, d_skip, norm_w, w_out, need_ctx_out):
    d_inner = norm_w.shape[0]
    conv_dim = conv_w.shape[1]
    heads = d_inner // SSD_HEADDIM
    w_main = w_in[:, :d_inner + conv_dim].astype(BF16)
    w_dt = jnp.pad(w_in[:, d_inner + conv_dim:], ((0, 0), (0, LANES - 2 * heads))).astype(BF16)
    bias = jnp.pad(dt_bias.reshape(1, 2 * heads), ((0, 0), (0, LANES - 2 * heads)))
    alog = jnp.pad(a_log.reshape(1, 2 * heads), ((0, 0), (0, LANES - 2 * heads)))
    d_exp = jnp.repeat(d_skip, SSD_HEADDIM).reshape(1, d_inner)
    nw = norm_w.reshape(1, d_inner)
    w_out = w_out.astype(BF16)
    slabs = conv_dim // LANES
    conv_w = jnp.pad(conv_w, ((0, SUBLANES - SSD_CONV_W), (0, 0))).reshape(SUBLANES, slabs, LANES).transpose(1, 0, 2)
    conv_b = conv_b.reshape(slabs, 1, LANES)

    outs = []
    s_f = s_b = None
    for name, s in streams:
        if s_f is None:
            s_f = s_b = jnp.zeros((s["batch"], SSD_STATE, d_inner), F32)
        z, xc, dt_raw = _ssd_inproj(s["h"], norm_g, s["mod"][:, 0:2], w_main, w_dt, conv_w, conv_b,
                                    seq=s["seq"], tm=s["tm"], d_inner=d_inner, chunk=1024)
        y_f, s_f = _ssd_scan(xc, dt_raw, bias, alog, s_f, batch=s["batch"], seq=s["seq"], reverse=False,
                             d_inner=d_inner)
        gn, s_b = _ssd_scan(xc, dt_raw, bias, alog, s_b, batch=s["batch"], seq=s["seq"], reverse=True,
                            d_inner=d_inner, finish_args=(y_f, z, d_exp, nw))
        if name == "ctx" and not need_ctx_out:
            outs.append(None)
        else:
            outs.append(_outproj(gn, w_out, s["h"], s["mod"][:, 2:3], seq=s["seq"], tm=s["tm"]))
    return outs


def _attn_mixer(streams, norm_g, w_qkv, sink, w_o, tables, need_ctx_out):
    (_, sc), (_, sl) = streams
    q_heads = sink.shape[0]
    qd = q_heads * ATTN_HEAD_DIM
    kd = ATTN_KV_HEADS * ATTN_HEAD_DIM
    w_qkv = w_qkv.astype(BF16)
    w_o = w_o.astype(BF16)
    qkv_c = _qkv_proj(sc["h"], norm_g, sc["mod"][:, 0:2], w_qkv, None, seq=sc["seq"], tm=sc["tm"],
                      q_cols=qd, k_cols=kd)
    qkv_l = _qkv_proj(sl["h"], norm_g, sl["mod"][:, 0:2], w_qkv, tables, seq=sl["seq"], tm=sl["tm"],
                      q_cols=qd, k_cols=kd)
    o_l = _attention(sink, qkv_l, qkv_c, batch=sl["batch"], seq=sl["seq"], n_ctx=sc["seq"], q_heads=q_heads,
                     band=True)
    h_l = _outproj(o_l, w_o, sl["h"], sl["mod"][:, 2:3], seq=sl["seq"], tm=sl["tm"])
    h_c = None
    if need_ctx_out:
        o_c = _attention(sink, qkv_c, qkv_c, batch=sc["batch"], seq=sc["seq"], n_ctx=sc["seq"], q_heads=q_heads,
                         band=False)
        h_c = _outproj(o_c, w_o, sc["h"], sc["mod"][:, 2:3], seq=sc["seq"], tm=sc["tm"])
    return [h_c, h_l]


def _gmlp_mixer(streams, norm_g, w_in, ln_g, ln_b, w_s, b_s, w_out, need_ctx_out):
    width = ln_g.shape[0]
    w_in = w_in.astype(BF16)
    w_out = w_out.astype(BF16)
    w_s = w_s.astype(BF16)
    outs = []
    for name, s in streams:
        if name == "ctx" and not need_ctx_out:
            outs.append(None)
            continue
        outs.append(_gmlp(s["h"], norm_g, s["mod"][:, 0:3], w_in, ln_g.reshape(1, width), ln_b.reshape(1, width),
                          w_s, b_s.T, w_out, seq=s["seq"], tm=s["tm"], chunk=1024))
    return outs


def kernel(x, c, ctx, c_ctx, w_mod, b_mod, norm_g, final_g, ssd_w_in, ssd_conv_w, ssd_conv_b, ssd_a_log, ssd_dt_bias, ssd_d, ssd_norm_w, ssd_w_out, attn_w_qkv, attn_sink, attn_w_o, gmlp_w_in, gmlp_ln_g, gmlp_ln_b, gmlp_w_s, gmlp_b_s, gmlp_w_out, ffn_w_in, ffn_w_out):
    batch, seq, d = x.shape
    n_ctx = ctx.shape[1]
    depth = w_mod.shape[0]
    hidden = ffn_w_out.shape[1]

    pad_rows = -(batch + 1) % 16
    cc = jnp.concatenate([c, c_ctx[None, :], jnp.zeros((pad_rows, d), F32)], axis=0)
    mods = _modulation(cc, w_mod, b_mod)
    tables = _rope_tables(seq)

    h_lat = x.reshape(batch * seq, d)
    h_ctx = ctx.reshape(batch * n_ctx, d)
    tm_lat = 512 if seq % 512 == 0 else 256
    tm_ctx = 256
    ffn_chunk = hidden

    for i in range(depth):
        last = i == depth - 1
        kind, j = i % 3, i // 3
        mod_lat = mods[i, :batch].reshape(batch, N_MOD, d)
        mod_ctx = mods[i, batch:batch + 1].reshape(1, N_MOD, d)
        g1 = norm_g[i, 0].reshape(1, d)
        g2 = norm_g[i, 1].reshape(1, d)
        streams = [
            ("ctx", dict(h=h_ctx, mod=mod_ctx, batch=batch, seq=n_ctx, tm=tm_ctx)),
            ("lat", dict(h=h_lat, mod=mod_lat, batch=batch, seq=seq, tm=tm_lat)),
        ]
        need_ctx_out = not last
        if kind == 0:
            h_ctx_new, h_lat = _ssd_mixer(streams, g1, ssd_w_in[j], ssd_conv_w[j], ssd_conv_b[j], ssd_a_log[j],
                                          ssd_dt_bias[j], ssd_d[j], ssd_norm_w[j], ssd_w_out[j], need_ctx_out)
        elif kind == 1:
            h_ctx_new, h_lat = _attn_mixer(streams, g1, attn_w_qkv[j], attn_sink[j], attn_w_o[j], tables,
                                           need_ctx_out)
        else:
            h_ctx_new, h_lat = _gmlp_mixer(streams, g1, gmlp_w_in[j], gmlp_ln_g[j], gmlp_ln_b[j], gmlp_w_s[j],
                                           gmlp_b_s[j], gmlp_w_out[j], need_ctx_out)

        w_in = ffn_w_in[i].astype(BF16)
        w_out = ffn_w_out[i].astype(BF16)
        h_lat = _ffn(h_lat, g2, mod_lat[:, 3:6], w_in, w_out, seq=seq, tm=tm_lat, chunk=ffn_chunk,
                     final_g=final_g.reshape(1, d) if last else None)
        if need_ctx_out:
            h_ctx = _ffn(h_ctx_new, g2, mod_ctx[:, 3:6], w_in, w_out, seq=n_ctx, tm=tm_ctx, chunk=ffn_chunk)
    return h_lat.reshape(batch, seq, d)
```

```python
import functools
import math

import jax
import jax.numpy as jnp
import numpy as np
from jax import lax
from jax.experimental import pallas as pl
from jax.experimental.pallas import tpu as pltpu

F32 = jnp.float32
BF16 = jnp.bfloat16

N_MOD = 6
GRID_W = 64
SSD_HEADDIM = 64
SSD_GROUPS = 4
SSD_STATE = 128
SSD_CONV_W = 5
SSD_CHUNK = 128
ATTN_HEAD_DIM = 64
ATTN_KV_HEADS = 4
ATTN_WINDOW = 128
ATTN_BLOCK = 128
ROPE_BASE = 10000.0
GMLP_GROUPS = 8
GMLP_CHUNK = 128
RMS_EPS = 1e-6
LN_EPS = 1e-5

LANES = 128
SUBLANES = 8
VMEM_LIMIT = 56 * 1024 * 1024


def _cparams(*sem):
    return pltpu.CompilerParams(dimension_semantics=sem, vmem_limit_bytes=VMEM_LIMIT)


def _silu(x):
    return x * (1.0 / (1.0 + jnp.exp(-x)))


def _norm_mod(x, g, shift, scale):
    y = x * lax.rsqrt(jnp.mean(x * x, axis=-1, keepdims=True) + RMS_EPS)
    return (y * g) * (1.0 + scale) + shift


def _split3(x):
    hi = x.astype(BF16)
    r1 = x - hi.astype(F32)
    mid = r1.astype(BF16)
    lo = (r1 - mid.astype(F32)).astype(BF16)
    return hi, mid, lo


def _mod_kernel(c_ref, w_ref, b_ref, o_ref):
    x = _silu(c_ref[...])
    w = w_ref[...]
    xh = x.astype(BF16)
    xl = (x - xh.astype(F32)).astype(BF16)
    wh = w.astype(BF16)
    wl = (w - wh.astype(F32)).astype(BF16)
    dot = functools.partial(jnp.dot, preferred_element_type=F32)
    o_ref[...] = dot(xh, wh) + (dot(xh, wl) + dot(xl, wh)) + b_ref[...]


def _modulation(cc, w_mod, b_mod):
    depth, d, n = w_mod.shape
    rows = cc.shape[0]
    tn = 1536
    return pl.pallas_call(
        _mod_kernel,
        out_shape=jax.ShapeDtypeStruct((depth, rows, n), F32),
        grid=(depth, n // tn),
        in_specs=[
            pl.BlockSpec((rows, d), lambda l, j: (0, 0)),
            pl.BlockSpec((None, d, tn), lambda l, j: (l, 0, j)),
            pl.BlockSpec((None, 1, tn), lambda l, j: (l, 0, j)),
        ],
        out_specs=pl.BlockSpec((None, rows, tn), lambda l, j: (l, 0, j)),
        compiler_params=_cparams("parallel", "parallel"),
        name="modulation",
    )(cc, w_mod, b_mod.reshape(depth, 1, n))


def _qkv_kernel(x_ref, g_ref, mod_ref, w_ref, *refs, rope, n_rot, q_cols, scale):
    if rope:
        cos_ref, sa_ref, sb_ref, o_ref = refs
    else:
        (o_ref,) = refs
    a = _norm_mod(x_ref[...], g_ref[...], mod_ref[0:1, :], mod_ref[1:2, :]).astype(BF16)
    y = jnp.dot(a, w_ref[...], preferred_element_type=F32)
    n = y.shape[1]
    if rope:
        cos, sa, sb = cos_ref[...], sa_ref[...], sb_ref[...]
    for s in range(n // LANES):
        ys = y[:, s * LANES:(s + 1) * LANES]
        if rope and s < n_rot:
            ys = ys * cos + pltpu.roll(ys, LANES - 16, 1) * sa + pltpu.roll(ys, 16, 1) * sb
        if s * LANES < q_cols:
            ys = ys * scale
        o_ref[:, s * LANES:(s + 1) * LANES] = ys.astype(o_ref.dtype)


def _qkv_proj(h, g, mod, w, tables, *, seq, tm, q_cols, k_cols):
    t, d = h.shape
    n = w.shape[1]
    tiles_per_seq = seq // tm
    bm = mod.shape[0]
    mod_idx = (lambda i: (i // tiles_per_seq, 0, 0)) if bm > 1 else (lambda i: (0, 0, 0))
    in_specs = [
        pl.BlockSpec((tm, d), lambda i: (i, 0)),
        pl.BlockSpec((1, d), lambda i: (0, 0)),
        pl.BlockSpec((None, 2, d), mod_idx),
        pl.BlockSpec((d, n), lambda i: (0, 0)),
    ]
    args = [h, g, mod, w]
    rope = tables is not None
    if rope:
        for tb in tables:
            in_specs.append(pl.BlockSpec((tm, LANES), lambda i: (i % tiles_per_seq, 0)))
            args.append(tb)
    return pl.pallas_call(
        functools.partial(_qkv_kernel, rope=rope, n_rot=(q_cols + k_cols) // LANES, q_cols=q_cols,
                          scale=ATTN_HEAD_DIM ** -0.5),
        out_shape=jax.ShapeDtypeStruct((t, n), BF16),
        grid=(t // tm,),
        in_specs=in_specs,
        out_specs=pl.BlockSpec((tm, n), lambda i: (i, 0)),
        compiler_params=_cparams("parallel"),
        name="qkv_proj",
    )(*args)


def _outproj_kernel(a_ref, w_ref, h_ref, mod_ref, o_ref):
    y = jnp.dot(a_ref[...], w_ref[...], preferred_element_type=F32)
    o_ref[...] = h_ref[...] + mod_ref[...] * y


def _outproj(a, w, h, gate, *, seq, tm):
    t, k = a.shape
    d = w.shape[1]
    tiles_per_seq = seq // tm
    bm = gate.shape[0]
    mod_idx = (lambda i: (i // tiles_per_seq, 0, 0)) if bm > 1 else (lambda i: (0, 0, 0))
    return pl.pallas_call(
        _outproj_kernel,
        out_shape=jax.ShapeDtypeStruct((t, d), F32),
        grid=(t // tm,),
        in_specs=[
            pl.BlockSpec((tm, k), lambda i: (i, 0)),
            pl.BlockSpec((k, d), lambda i: (0, 0)),
            pl.BlockSpec((tm, d), lambda i: (i, 0)),
            pl.BlockSpec((None, 1, d), mod_idx),
        ],
        out_specs=pl.BlockSpec((tm, d), lambda i: (i, 0)),
        compiler_params=_cparams("parallel"),
        name="outproj",
    )(a, w, h, gate)


def _ffn_kernel(x_ref, g_ref, mod_ref, win_ref, wo_ref, *refs, final, hidden, chunk):
    if final:
        fg_ref, o_ref = refs
    else:
        (o_ref,) = refs
    x = x_ref[...]
    a = _norm_mod(x, g_ref[...], mod_ref[0:1, :], mod_ref[1:2, :]).astype(BF16)
    dot = functools.partial(jnp.dot, preferred_element_type=F32)
    acc = None
    for c0 in range(0, hidden, chunk):
        gte = dot(a, win_ref[:, c0:c0 + chunk])
        up = dot(a, win_ref[:, hidden + c0:hidden + c0 + chunk])
        part = dot((_silu(gte) * up).astype(BF16), wo_ref[c0:c0 + chunk, :])
        acc = part if acc is None else acc + part
    y = x + mod_ref[2:3, :] * acc
    if final:
        y = y * lax.rsqrt(jnp.mean(y * y, axis=-1, keepdims=True) + RMS_EPS) * fg_ref[...]
    o_ref[...] = y


def _ffn(h, g, mod, w_in, w_out, *, seq, tm, chunk, final_g=None):
    t, d = h.shape
    hidden = w_out.shape[0]
    tiles_per_seq = seq // tm
    bm = mod.shape[0]
    mod_idx = (lambda i: (i // tiles_per_seq, 0, 0)) if bm > 1 else (lambda i: (0, 0, 0))
    resident = pl.Buffered(1)
    in_specs = [
        pl.BlockSpec((tm, d), lambda i: (i, 0)),
        pl.BlockSpec((1, d), lambda i: (0, 0)),
        pl.BlockSpec((None, 3, d), mod_idx),
        pl.BlockSpec((d, 2 * hidden), lambda i: (0, 0), pipeline_mode=resident),
        pl.BlockSpec((hidden, d), lambda i: (0, 0), pipeline_mode=resident),
    ]
    args = [h, g, mod, w_in, w_out]
    if final_g is not None:
        in_specs.append(pl.BlockSpec((1, d), lambda i: (0, 0)))
        args.append(final_g)
    return pl.pallas_call(
        functools.partial(_ffn_kernel, final=final_g is not None, hidden=hidden, chunk=chunk),
        out_shape=jax.ShapeDtypeStruct((t, d), F32),
        grid=(t // tm,),
        in_specs=in_specs,
        out_specs=pl.BlockSpec((tm, d), lambda i: (i, 0)),
        compiler_params=_cparams("parallel"),
        name="ffn",
    )(*args)


HALO = 16


def _ssd_inproj_kernel(x_ref, xp_ref, xn_ref, g_ref, mod_ref, w_ref, wdt_ref, cw_ref, cb_ref,
                       z_ref, xc_ref, dt_ref, a_scr, *, tiles_per_seq, tm, d_inner, chunk):
    pos = pl.program_id(0) % tiles_per_seq
    g, shift, scale = g_ref[...], mod_ref[0:1, :], mod_ref[1:2, :]
    a_prev = jnp.where(pos == 0, 0.0, _norm_mod(xp_ref[...], g, shift, scale))
    a_next = jnp.where(pos == tiles_per_seq - 1, 0.0, _norm_mod(xn_ref[...], g, shift, scale))
    a_scr[0:HALO, :] = a_prev.astype(BF16)
    a_scr[HALO:HALO + tm, :] = _norm_mod(x_ref[...], g, shift, scale).astype(BF16)
    a_scr[HALO + tm:, :] = a_next.astype(BF16)
    dot = functools.partial(jnp.dot, preferred_element_type=F32)
    a_main = a_scr[HALO:HALO + tm, :]
    dt_ref[...] = dot(a_main, wdt_ref[...])

    rows = tm + 2 * HALO
    half = SSD_CONV_W // 2
    spc = chunk // LANES
    t0, nt = HALO // SUBLANES, tm // SUBLANES
    sub = lax.broadcasted_iota(jnp.int32, (1, SUBLANES, LANES), 1)
    for c0 in range(0, d_inner, chunk):
        z_ref[:, c0:c0 + chunk] = dot(a_main, w_ref[:, c0:c0 + chunk])
    for c in range(cw_ref.shape[0] // spc):
        pre = dot(a_scr[...], w_ref[:, d_inner + c * chunk:d_inner + (c + 1) * chunk])
        for s in range(spc):
            e = pre[:, s * LANES:(s + 1) * LANES].reshape(rows // SUBLANES, SUBLANES, LANES)
            cw = cw_ref[c * spc + s]
            acc = cb_ref[c * spc + s] + cw[half:half + 1, :] * e[t0:t0 + nt]
            rot = {1: pltpu.roll(e, 1, 1), 2: pltpu.roll(e, 2, 1)}
            rot[6] = pltpu.roll(rot[2], 4, 1)
            rot[7] = pltpu.roll(rot[6], 1, 1)
            for k in range(SSD_CONV_W):
                sh = half - k
                if sh == 0:
                    continue
                r = rot[sh % SUBLANES]
                if sh > 0:
                    tap = jnp.where(sub < sh, r[t0 - 1:t0 - 1 + nt], r[t0:t0 + nt])
                else:
                    tap = jnp.where(sub < SUBLANES + sh, r[t0:t0 + nt], r[t0 + 1:t0 + 1 + nt])
                acc = acc + cw[k:k + 1, :] * tap
            xc_ref[c * spc + s] = _silu(acc).reshape(tm, LANES)


def _ssd_inproj(h, g, mod, w_main, w_dt, conv_w, conv_b, *, seq, tm, d_inner, chunk):
    t, d = h.shape
    slabs = conv_w.shape[0]
    tiles_per_seq = seq // tm
    bm = mod.shape[0]
    mod_idx = (lambda i: (i // tiles_per_seq, 0, 0)) if bm > 1 else (lambda i: (0, 0, 0))
    rb = tm // HALO
    last_rb = t // HALO - 1
    resident = pl.Buffered(1)
    return pl.pallas_call(
        functools.partial(_ssd_inproj_kernel, tiles_per_seq=tiles_per_seq, tm=tm, d_inner=d_inner, chunk=chunk),
        out_shape=[jax.ShapeDtypeStruct((t, d_inner), F32),
                   jax.ShapeDtypeStruct((slabs, t, LANES), F32),
                   jax.ShapeDtypeStruct((t, w_dt.shape[1]), F32)],
        grid=(t // tm,),
        in_specs=[
            pl.BlockSpec((tm, d), lambda i: (i, 0)),
            pl.BlockSpec((HALO, d), lambda i: (jnp.maximum(i * rb - 1, 0), 0)),
            pl.BlockSpec((HALO, d), lambda i: (jnp.minimum((i + 1) * rb, last_rb), 0)),
            pl.BlockSpec((1, d), lambda i: (0, 0)),
            pl.BlockSpec((None, 2, d), mod_idx),
            pl.BlockSpec(w_main.shape, lambda i: (0, 0), pipeline_mode=resident),
            pl.BlockSpec(w_dt.shape, lambda i: (0, 0), pipeline_mode=resident),
            pl.BlockSpec(conv_w.shape, lambda i: (0, 0, 0)),
            pl.BlockSpec(conv_b.shape, lambda i: (0, 0, 0)),
        ],
        out_specs=[pl.BlockSpec((tm, d_inner), lambda i: (i, 0)),
                   pl.BlockSpec((slabs, tm, LANES), lambda i: (0, i, 0)),
                   pl.BlockSpec((tm, w_dt.shape[1]), lambda i: (i, 0))],
        scratch_shapes=[pltpu.VMEM((tm + 2 * HALO, d), BF16)],
        compiler_params=_cparams("parallel"),
        name="ssd_inproj",
    )(h, h, h, g, mod, w_main, w_dt, conv_w, conv_b)


def _ssd_kernel(x_ref, b_ref, c_ref, dt_ref, bias_ref, alog_ref, s0_ref, *refs,
                reverse, finish, heads, col_off):
    if finish:
        yf_ref, z_ref, d_ref, nw_ref, o_ref, sfin_ref, state, y_scr = refs
    else:
        o_ref, sfin_ref, state = refs
        y_scr = o_ref
    L = SSD_CHUNK
    hd = SSD_HEADDIM
    hpg = heads // SSD_GROUPS
    gw = hpg * hd
    pairs = gw // LANES
    j = pl.program_id(1)

    @pl.when(j == 0)
    def _():
        state[...] = s0_ref[...]

    row = lax.broadcasted_iota(jnp.int32, (L, L), 0)
    col = lax.broadcasted_iota(jnp.int32, (L, L), 1)
    mask = (col >= row) if reverse else (col <= row)
    tri = jnp.where(mask, 1.0, 0.0).astype(BF16)
    first = col < hd
    first_row = lax.broadcasted_iota(jnp.int32, (1, LANES), 1) < hd

    v = dt_ref[...] + bias_ref[...]
    dt = jnp.maximum(v, 0.0) + jnp.log1p(jnp.exp(-jnp.abs(v)))
    dta = dt * (-jnp.exp(alog_ref[...]))
    p1, p2, p3 = _split3(dta)
    dot = functools.partial(jnp.dot, preferred_element_type=F32)
    a_cum = dot(tri, p1) + (dot(tri, p2) + dot(tri, p3))
    b_log = a_cum - jnp.log(dt)
    a_tot = a_cum[0:1, :] if reverse else a_cum[L - 1:L, :]
    b_t = b_log.T
    dtw_t = jnp.exp(a_tot - b_log).T
    e_tot = jnp.exp(a_tot)

    for g in range(SSD_GROUPS):
        bg = b_ref[g]
        cg = c_ref[g].astype(BF16)
        cb = lax.dot_general(cg, bg.astype(BF16), (((1,), (1,)), ((), ())), preferred_element_type=F32)
        bg_t = bg.T
        y_in = dot(cg, state[:, g * gw:(g + 1) * gw].astype(BF16))
        for k in range(pairs):
            ca = col_off + g * hpg + 2 * k
            lo = g * gw + k * LANES
            xp = x_ref[g * pairs + k]
            x_ab = jnp.concatenate([jnp.where(first, xp, 0.0), jnp.where(first, 0.0, xp)], axis=0).astype(BF16)
            a_a = jnp.broadcast_to(a_cum[:, ca:ca + 1], (L, L))
            a_b = jnp.broadcast_to(a_cum[:, ca + 1:ca + 2], (L, L))
            m_a = (cb * jnp.exp(jnp.where(mask, a_a - b_t[ca:ca + 1, :], -jnp.inf))).astype(BF16)
            m_b = (cb * jnp.exp(jnp.where(mask, a_b - b_t[ca + 1:ca + 2, :], -jnp.inf))).astype(BF16)
            e_cum = jnp.exp(jnp.where(first, a_a, a_b))
            y_pair = dot(jnp.concatenate([m_a, m_b], axis=1), x_ab) + y_in[:, k * LANES:(k + 1) * LANES] * e_cum
            if finish:
                y_pair = y_pair + xp * d_ref[:, lo:lo + LANES]
            y_scr[:, lo:lo + LANES] = y_pair
            bt_a = (bg_t * dtw_t[ca:ca + 1, :]).astype(BF16)
            bt_b = (bg_t * dtw_t[ca + 1:ca + 2, :]).astype(BF16)
            e_end = jnp.where(first_row, e_tot[:, ca:ca + 1], e_tot[:, ca + 1:ca + 2])
            state[:, lo:lo + LANES] = (state[:, lo:lo + LANES] * e_end
                                       + dot(jnp.concatenate([bt_a, bt_b], axis=1), x_ab))

    if finish:
        y = y_scr[...] + yf_ref[...]
        gated = y * _silu(z_ref[...])
        gn = gated * lax.rsqrt(jnp.mean(gated * gated, axis=-1, keepdims=True) + RMS_EPS) * nw_ref[...]
        o_ref[...] = gn.astype(o_ref.dtype)

    @pl.when(j == pl.num_programs(1) - 1)
    def _():
        sfin_ref[...] = state[...]


def _ssd_scan(xc, dt_raw, dt_bias, a_log, state0, *, batch, seq, reverse, d_inner, finish_args=None):
    t = xc.shape[1]
    L = SSD_CHUNK
    nc = seq // L
    heads = d_inner // SSD_HEADDIM
    x_slabs = d_inner // LANES
    gs = SSD_GROUPS * SSD_STATE // LANES
    assert SSD_STATE == LANES and x_slabs % gs == 0
    finish = finish_args is not None

    def rows(b, j):
        return b * nc + ((nc - 1 - j) if reverse else j)

    in_specs = [
        pl.BlockSpec((x_slabs, L, LANES), lambda b, j: (0, rows(b, j), 0)),
        pl.BlockSpec((gs, L, LANES), lambda b, j: (x_slabs // gs, rows(b, j), 0)),
        pl.BlockSpec((gs, L, LANES), lambda b, j: (x_slabs // gs + 1, rows(b, j), 0)),
        pl.BlockSpec((L, LANES), lambda b, j: (rows(b, j), 0)),
        pl.BlockSpec((1, LANES), lambda b, j: (0, 0)),
        pl.BlockSpec((1, LANES), lambda b, j: (0, 0)),
        pl.BlockSpec((None, SSD_STATE, d_inner), lambda b, j: (b, 0, 0)),
    ]
    args = [xc, xc, xc, dt_raw, dt_bias, a_log, state0]
    if finish:
        y_f, z, d_exp, norm_w = finish_args
        in_specs += [
            pl.BlockSpec((L, d_inner), lambda b, j: (rows(b, j), 0)),
            pl.BlockSpec((L, d_inner), lambda b, j: (rows(b, j), 0)),
            pl.BlockSpec((1, d_inner), lambda b, j: (0, 0)),
            pl.BlockSpec((1, d_inner), lambda b, j: (0, 0)),
        ]
        args += [y_f, z, d_exp, norm_w]
    out_dtype = BF16 if finish else F32
    return pl.pallas_call(
        functools.partial(_ssd_kernel, reverse=reverse, finish=finish, heads=heads,
                          col_off=heads if reverse else 0),
        out_shape=[jax.ShapeDtypeStruct((t, d_inner), out_dtype),
                   jax.ShapeDtypeStruct((batch, SSD_STATE, d_inner), F32)],
        grid=(batch, nc),
        in_specs=in_specs,
        out_specs=[pl.BlockSpec((L, d_inner), lambda b, j: (rows(b, j), 0)),
                   pl.BlockSpec((None, SSD_STATE, d_inner), lambda b, j: (b, 0, 0))],
        scratch_shapes=[pltpu.VMEM((SSD_STATE, d_inner), F32)]
        + ([pltpu.VMEM((L, d_inner), F32)] if finish else []),
        compiler_params=_cparams("parallel", "arbitrary"),
        name="ssd_scan_bwd" if reverse else "ssd_scan_fwd",
    )(*args)


def _attn_kernel(sink_ref, q_ref, kc_ref, vc_ref, wo_ref, h_ref, gate_ref, *refs, band, q_heads):
    if band:
        kp_ref, k0_ref, kn_ref, vp_ref, v0_ref, vn_ref, o_ref = refs
    else:
        (o_ref,) = refs
    hd = ATTN_HEAD_DIM
    group = q_heads // ATTN_KV_HEADS
    blk = q_ref.shape[0]
    dot = functools.partial(jnp.dot, preferred_element_type=F32)
    dot_t = lambda a, b: lax.dot_general(a, b, (((1,), (1,)), ((), ())), preferred_element_type=F32)
    gq = group * blk
    key_idx = lax.broadcasted_iota(jnp.int32, (blk, gq), 0)
    qry_idx = lax.broadcasted_iota(jnp.int32, (blk, gq), 1) % blk
    lane_head = lax.broadcasted_iota(jnp.int32, (1, gq), 1) // blk
    segs = [(kc_ref[c0:c0 + blk, :], vc_ref[c0:c0 + blk, :], None) for c0 in range(0, kc_ref.shape[0], blk)]
    if band:
        j = pl.program_id(1)
        ok_prev = key_idx >= qry_idx + jnp.where(j > 0, 0, 2 * blk)
        ok_next = key_idx + jnp.where(j < pl.num_programs(1) - 1, 0, 2 * blk) <= qry_idx
        segs += [(kp_ref[...], vp_ref[...], ok_prev), (k0_ref[...], v0_ref[...], None),
                 (kn_ref[...], vn_ref[...], ok_next)]
    v_t = [v.astype(F32).T.astype(BF16) for _, v, _ in segs]
    o_rows = []
    for kv in range(ATTN_KV_HEADS):
        ksl = slice(kv * hd, (kv + 1) * hd)
        h0 = kv * group
        qs = jnp.concatenate([q_ref[:, (h0 + i) * hd:(h0 + i + 1) * hd] for i in range(group)], axis=0)
        sink = jnp.full((1, gq), sink_ref[h0], F32)
        for i in range(1, group):
            sink = jnp.where(lane_head == i, sink_ref[h0 + i], sink)
        s_tiles = []
        for k, _, ok in segs:
            s = dot_t(k[:, ksl], qs)
            s_tiles.append(s if ok is None else jnp.where(ok, s, -jnp.inf))
        m = jnp.maximum(jnp.max(functools.reduce(jnp.maximum, s_tiles), axis=0, keepdims=True), sink)
        p_tiles = [jnp.exp(s - m) for s in s_tiles]
        den = jnp.exp(sink - m) + jnp.sum(functools.reduce(jnp.add, p_tiles), axis=0, keepdims=True)
        o_t = functools.reduce(jnp.add, [dot(vt[ksl, :], p.astype(BF16)) for vt, p in zip(v_t, p_tiles)])
        o_t = o_t * (1.0 / den)
        o_rows += [o_t[:, i * blk:(i + 1) * blk] for i in range(group)]
    o = jnp.concatenate(o_rows, axis=0).T.astype(BF16)
    o_ref[...] = h_ref[...] + gate_ref[...] * dot(o, wo_ref[...])


def _attention(sink, qkv, qkv_ctx, w_o, h, gate, *, batch, seq, n_ctx, q_heads, band):
    blk = ATTN_BLOCK
    assert n_ctx % blk == 0 and seq % blk == 0
    nb = seq // blk
    d = h.shape[1]
    qd = q_heads * ATTN_HEAD_DIM
    kd = ATTN_KV_HEADS * ATTN_HEAD_DIM
    kcol, vcol = qd // kd, qd // kd + 1
    gate_idx = (lambda b, j: (b, 0, 0)) if gate.shape[0] > 1 else (lambda b, j: (0, 0, 0))
    in_specs = [
        pl.BlockSpec(memory_space=pltpu.SMEM),
        pl.BlockSpec((blk, qd), lambda b, j: (b * nb + j, 0)),
        pl.BlockSpec((n_ctx, kd), lambda b, j: (b, kcol)),
        pl.BlockSpec((n_ctx, kd), lambda b, j: (b, vcol)),
        pl.BlockSpec(w_o.shape, lambda b, j: (0, 0), pipeline_mode=pl.Buffered(1)),
        pl.BlockSpec((blk, d), lambda b, j: (b * nb + j, 0)),
        pl.BlockSpec((None, 1, d), gate_idx),
    ]
    args = [sink, qkv, qkv_ctx, qkv_ctx, w_o, h, gate]
    if band:
        prev = lambda b, j: b * nb + jnp.maximum(j - 1, 0)
        cur = lambda b, j: b * nb + j
        nxt = lambda b, j: b * nb + jnp.minimum(j + 1, nb - 1)
        for colb in (kcol, vcol):
            for f in (prev, cur, nxt):
                in_specs.append(pl.BlockSpec((blk, kd), functools.partial(lambda b, j, f, colb: (f(b, j), colb),
                                                                          f=f, colb=colb)))
                args.append(qkv)
    return pl.pallas_call(
        functools.partial(_attn_kernel, band=band, q_heads=q_heads),
        out_shape=jax.ShapeDtypeStruct((batch * seq, d), F32),
        grid=(batch, nb),
        in_specs=in_specs,
        out_specs=pl.BlockSpec((blk, d), lambda b, j: (b * nb + j, 0)),
        compiler_params=_cparams("parallel", "parallel"),
        name="window_attn" if band else "ctx_attn",
    )(*args)


def _gmlp_kernel(x_ref, g_ref, mod_ref, win_ref, lg_ref, lb_ref, ws_ref, bs_ref, wo_ref, o_ref, gm_scr, *, chunk):
    x = x_ref[...]
    a = _norm_mod(x, g_ref[...], mod_ref[0:1, :], mod_ref[1:2, :]).astype(BF16)
    width = wo_ref.shape[0]
    dot = functools.partial(jnp.dot, preferred_element_type=F32)

    def gelu_proj(c0):
        y = dot(a, win_ref[:, c0:c0 + chunk])
        return 0.5 * y * (1.0 + lax.erf(y * (1.0 / math.sqrt(2.0))))

    v = jnp.concatenate([gelu_proj(width + c0) for c0 in range(0, width, chunk)], axis=1)
    mu = jnp.mean(v, axis=-1, keepdims=True)
    vc = v - mu
    var = jnp.mean(vc * vc, axis=-1, keepdims=True)
    vn = (vc * lax.rsqrt(var + LN_EPS) * lg_ref[...] + lb_ref[...]).astype(BF16)
    gd = width // GMLP_GROUPS
    ch = GMLP_CHUNK
    for c0 in range(0, width, chunk):
        u = gelu_proj(c0)
        for g in range(c0 // gd, (c0 + chunk) // gd):
            for r in range(0, x.shape[0], ch):
                sv = dot(ws_ref[g], vn[r:r + ch, g * gd:(g + 1) * gd]) + bs_ref[:, g:g + 1]
                gm_scr[r:r + ch, g * gd:(g + 1) * gd] = (u[r:r + ch, g * gd - c0:(g + 1) * gd - c0] * sv).astype(BF16)
    o_ref[...] = x + mod_ref[2:3, :] * dot(gm_scr[...], wo_ref[...])


def _gmlp(h, g, mod, w_in, ln_g, ln_b, w_s, b_s_t, w_out, *, seq, tm, chunk):
    t, d = h.shape
    width = w_out.shape[0]
    assert tm % GMLP_CHUNK == 0 and chunk % (width // GMLP_GROUPS) == 0
    tiles_per_seq = seq // tm
    bm = mod.shape[0]
    mod_idx = (lambda i: (i // tiles_per_seq, 0, 0)) if bm > 1 else (lambda i: (0, 0, 0))
    resident = pl.Buffered(1)
    return pl.pallas_call(
        functools.partial(_gmlp_kernel, chunk=chunk),
        out_shape=jax.ShapeDtypeStruct((t, d), F32),
        grid=(t // tm,),
        in_specs=[
            pl.BlockSpec((tm, d), lambda i: (i, 0)),
            pl.BlockSpec((1, d), lambda i: (0, 0)),
            pl.BlockSpec((None, 3, d), mod_idx),
            pl.BlockSpec(w_in.shape, lambda i: (0, 0), pipeline_mode=resident),
            pl.BlockSpec((1, width), lambda i: (0, 0)),
            pl.BlockSpec((1, width), lambda i: (0, 0)),
            pl.BlockSpec(w_s.shape, lambda i: (0, 0, 0)),
            pl.BlockSpec(b_s_t.shape, lambda i: (0, 0)),
            pl.BlockSpec(w_out.shape, lambda i: (0, 0), pipeline_mode=resident),
        ],
        out_specs=pl.BlockSpec((tm, d), lambda i: (i, 0)),
        scratch_shapes=[pltpu.VMEM((tm, width), BF16)],
        compiler_params=_cparams("parallel"),
        name="gmlp",
    )(h, g, mod, w_in, ln_g, ln_b, w_s, b_s_t, w_out)


def _rope_tables(n):
    freqs = ATTN_HEAD_DIM // 4
    rows = n // GRID_W
    row = jnp.repeat(jnp.arange(rows, dtype=jnp.int32), GRID_W, total_repeat_length=n)
    col = jnp.tile(jnp.arange(GRID_W, dtype=jnp.int32), rows)
    inv = ROPE_BASE ** (-jnp.arange(freqs, dtype=F32) / freqs)
    ang = jnp.stack([row.astype(F32)[:, None] * inv, col.astype(F32)[:, None] * inv], axis=1)
    ang = jnp.repeat(ang[:, :, None, :], 2, axis=2).reshape(n, ATTN_HEAD_DIM)
    cos, sin = jnp.cos(ang), jnp.sin(ang)
    reps = LANES // ATTN_HEAD_DIM
    cos, sin = jnp.tile(cos, (1, reps)), jnp.tile(sin, (1, reps))
    first_half = jnp.asarray((np.arange(LANES) // freqs) % 2 == 0)[None, :]
    return cos, jnp.where(first_half, -sin, 0.0), jnp.where(first_half, 0.0, sin)


def _ssd_mixer(streams, norm_g, w_in, conv_w, conv_b, a_log, dt_bias, d_skip, norm_w, w_out, need_ctx_out):
    d_inner = norm_w.shape[0]
    conv_dim = conv_w.shape[1]
    heads = d_inner // SSD_HEADDIM
    w_main = w_in[:, :d_inner + conv_dim].astype(BF16)
    w_dt = jnp.pad(w_in[:, d_inner + conv_dim:], ((0, 0), (0, LANES - 2 * heads))).astype(BF16)
    bias = jnp.pad(dt_bias.reshape(1, 2 * heads), ((0, 0), (0, LANES - 2 * heads)))
    alog = jnp.pad(a_log.reshape(1, 2 * heads), ((0, 0), (0, LANES - 2 * heads)))
    d_exp = jnp.repeat(d_skip, SSD_HEADDIM).reshape(1, d_inner)
    nw = norm_w.reshape(1, d_inner)
    w_out = w_out.astype(BF16)
    slabs = conv_dim // LANES
    conv_w = jnp.pad(conv_w, ((0, SUBLANES - SSD_CONV_W), (0, 0))).reshape(SUBLANES, slabs, LANES).transpose(1, 0, 2)
    conv_b = conv_b.reshape(slabs, 1, LANES)

    outs = []
    s_f = s_b = None
    for name, s in streams:
        if s_f is None:
            s_f = s_b = jnp.zeros((s["batch"], SSD_STATE, d_inner), F32)
        z, xc, dt_raw = _ssd_inproj(s["h"], norm_g, s["mod"][:, 0:2], w_main, w_dt, conv_w, conv_b,
                                    seq=s["seq"], tm=s["tm"], d_inner=d_inner, chunk=1024)
        y_f, s_f = _ssd_scan(xc, dt_raw, bias, alog, s_f, batch=s["batch"], seq=s["seq"], reverse=False,
                             d_inner=d_inner)
        gn, s_b = _ssd_scan(xc, dt_raw, bias, alog, s_b, batch=s["batch"], seq=s["seq"], reverse=True,
                            d_inner=d_inner, finish_args=(y_f, z, d_exp, nw))
        if name == "ctx" and not need_ctx_out:
            outs.append(None)
        else:
            tm_out = 2 * s["tm"] if s["seq"] % (2 * s["tm"]) == 0 else s["tm"]
            outs.append(_outproj(gn, w_out, s["h"], s["mod"][:, 2:3], seq=s["seq"], tm=tm_out))
    return outs


def _attn_mixer(streams, norm_g, w_qkv, sink, w_o, tables, need_ctx_out):
    (_, sc), (_, sl) = streams
    q_heads = sink.shape[0]
    qd = q_heads * ATTN_HEAD_DIM
    kd = ATTN_KV_HEADS * ATTN_HEAD_DIM
    w_qkv = w_qkv.astype(BF16)
    w_o = w_o.astype(BF16)
    qkv_c = _qkv_proj(sc["h"], norm_g, sc["mod"][:, 0:2], w_qkv, None, seq=sc["seq"], tm=sc["tm"],
                      q_cols=qd, k_cols=kd)
    qkv_l = _qkv_proj(sl["h"], norm_g, sl["mod"][:, 0:2], w_qkv, tables, seq=sl["seq"], tm=sl["tm"],
                      q_cols=qd, k_cols=kd)
    h_l = _attention(sink, qkv_l, qkv_c, w_o, sl["h"], sl["mod"][:, 2:3], batch=sl["batch"], seq=sl["seq"],
                     n_ctx=sc["seq"], q_heads=q_heads, band=True)
    h_c = None
    if need_ctx_out:
        h_c = _attention(sink, qkv_c, qkv_c, w_o, sc["h"], sc["mod"][:, 2:3], batch=sc["batch"], seq=sc["seq"],
                         n_ctx=sc["seq"], q_heads=q_heads, band=False)
    return [h_c, h_l]


def _gmlp_mixer(streams, norm_g, w_in, ln_g, ln_b, w_s, b_s, w_out, need_ctx_out):
    width = ln_g.shape[0]
    w_in = w_in.astype(BF16)
    w_out = w_out.astype(BF16)
    w_s = w_s.astype(BF16)
    outs = []
    for name, s in streams:
        if name == "ctx" and not need_ctx_out:
            outs.append(None)
            continue
        outs.append(_gmlp(s["h"], norm_g, s["mod"][:, 0:3], w_in, ln_g.reshape(1, width), ln_b.reshape(1, width),
                          w_s, b_s.T, w_out, seq=s["seq"], tm=s["tm"], chunk=1024))
    return outs


def kernel(x, c, ctx, c_ctx, w_mod, b_mod, norm_g, final_g, ssd_w_in, ssd_conv_w, ssd_conv_b, ssd_a_log, ssd_dt_bias, ssd_d, ssd_norm_w, ssd_w_out, attn_w_qkv, attn_sink, attn_w_o, gmlp_w_in, gmlp_ln_g, gmlp_ln_b, gmlp_w_s, gmlp_b_s, gmlp_w_out, ffn_w_in, ffn_w_out):
    batch, seq, d = x.shape
    n_ctx = ctx.shape[1]
    depth = w_mod.shape[0]
    hidden = ffn_w_out.shape[1]

    pad_rows = -(batch + 1) % 16
    cc = jnp.concatenate([c, c_ctx[None, :], jnp.zeros((pad_rows, d), F32)], axis=0)
    mods = _modulation(cc, w_mod, b_mod)
    tables = _rope_tables(seq)

    h_lat = x.reshape(batch * seq, d)
    h_ctx = ctx.reshape(batch * n_ctx, d)
    tm_lat = 512 if seq % 512 == 0 else 256
    tm_ctx = 256
    ffn_chunk = hidden

    for i in range(depth):
        last = i == depth - 1
        kind, j = i % 3, i // 3
        mod_lat = mods[i, :batch].reshape(batch, N_MOD, d)
        mod_ctx = mods[i, batch:batch + 1].reshape(1, N_MOD, d)
        g1 = norm_g[i, 0].reshape(1, d)
        g2 = norm_g[i, 1].reshape(1, d)
        streams = [
            ("ctx", dict(h=h_ctx, mod=mod_ctx, batch=batch, seq=n_ctx, tm=tm_ctx)),
            ("lat", dict(h=h_lat, mod=mod_lat, batch=batch, seq=seq, tm=tm_lat)),
        ]
        need_ctx_out = not last
        if kind == 0:
            h_ctx_new, h_lat = _ssd_mixer(streams, g1, ssd_w_in[j], ssd_conv_w[j], ssd_conv_b[j], ssd_a_log[j],
                                          ssd_dt_bias[j], ssd_d[j], ssd_norm_w[j], ssd_w_out[j], need_ctx_out)
        elif kind == 1:
            h_ctx_new, h_lat = _attn_mixer(streams, g1, attn_w_qkv[j], attn_sink[j], attn_w_o[j], tables,
                                           need_ctx_out)
        else:
            h_ctx_new, h_lat = _gmlp_mixer(streams, g1, gmlp_w_in[j], gmlp_ln_g[j], gmlp_ln_b[j], gmlp_w_s[j],
                                           gmlp_b_s[j], gmlp_w_out[j], need_ctx_out)

        w_in = ffn_w_in[i].astype(BF16)
        w_out = ffn_w_out[i].astype(BF16)
        h_lat = _ffn(h_lat, g2, mod_lat[:, 3:6], w_in, w_out, seq=seq, tm=tm_lat, chunk=ffn_chunk,
                     final_g=final_g.reshape(1, d) if last else None)
        if need_ctx_out:
            h_ctx = _ffn(h_ctx_new, g2, mod_ctx[:, 3:6], w_in, w_out, seq=n_ctx, tm=tm_ctx, chunk=ffn_chunk)
    return h_lat.reshape(batch, seq, d)
```

```python
import functools
import math

import jax
import jax.numpy as jnp
import numpy as np
from jax import lax
from jax.experimental import pallas as pl
from jax.experimental.pallas import tpu as pltpu

F32 = jnp.float32
BF16 = jnp.bfloat16

N_MOD = 6
GRID_W = 64
SSD_HEADDIM = 64
SSD_GROUPS = 4
SSD_STATE = 128
SSD_CONV_W = 5
SSD_CHUNK = 128
ATTN_HEAD_DIM = 64
ATTN_KV_HEADS = 4
ATTN_WINDOW = 128
ATTN_BLOCK = 128
ROPE_BASE = 10000.0
GMLP_GROUPS = 8
GMLP_CHUNK = 128
RMS_EPS = 1e-6
LOG2E = math.log2(math.e)
LN_EPS = 1e-5

LANES = 128
SUBLANES = 8
VMEM_LIMIT = 56 * 1024 * 1024


def _cparams(*sem):
    return pltpu.CompilerParams(dimension_semantics=sem, vmem_limit_bytes=VMEM_LIMIT)


def _silu(x):
    return x * (1.0 / (1.0 + jnp.exp2(x * -LOG2E)))


def _norm_mod(x, g, shift, scale):
    y = x * lax.rsqrt(jnp.mean(x * x, axis=-1, keepdims=True) + RMS_EPS)
    return (y * g) * (1.0 + scale) + shift


def _split3(x):
    hi = x.astype(BF16)
    r1 = x - hi.astype(F32)
    mid = r1.astype(BF16)
    lo = (r1 - mid.astype(F32)).astype(BF16)
    return hi, mid, lo


def _mod_kernel(c_ref, w_ref, b_ref, o_ref):
    x = _silu(c_ref[...])
    w = w_ref[...]
    xh = x.astype(BF16)
    xl = (x - xh.astype(F32)).astype(BF16)
    wh = w.astype(BF16)
    wl = (w - wh.astype(F32)).astype(BF16)
    dot = functools.partial(jnp.dot, preferred_element_type=F32)
    o_ref[...] = dot(xh, wh) + (dot(xh, wl) + dot(xl, wh)) + b_ref[...]


def _modulation(cc, w_mod, b_mod):
    depth, d, n = w_mod.shape
    rows = cc.shape[0]
    tn = 1536
    return pl.pallas_call(
        _mod_kernel,
        out_shape=jax.ShapeDtypeStruct((depth, rows, n), F32),
        grid=(depth, n // tn),
        in_specs=[
            pl.BlockSpec((rows, d), lambda l, j: (0, 0)),
            pl.BlockSpec((None, d, tn), lambda l, j: (l, 0, j)),
            pl.BlockSpec((None, 1, tn), lambda l, j: (l, 0, j)),
        ],
        out_specs=pl.BlockSpec((None, rows, tn), lambda l, j: (l, 0, j)),
        compiler_params=_cparams("parallel", "parallel"),
        name="modulation",
    )(cc, w_mod, b_mod.reshape(depth, 1, n))


def _qkv_kernel(x_ref, g_ref, mod_ref, w_ref, *refs, rope, n_rot, q_cols, scale):
    if rope:
        cos_ref, sa_ref, sb_ref, o_ref = refs
    else:
        (o_ref,) = refs
    a = _norm_mod(x_ref[...], g_ref[...], mod_ref[0:1, :], mod_ref[1:2, :]).astype(BF16)
    y = jnp.dot(a, w_ref[...], preferred_element_type=F32)
    n = y.shape[1]
    if rope:
        cos, sa, sb = cos_ref[...], sa_ref[...], sb_ref[...]
    for s in range(n // LANES):
        ys = y[:, s * LANES:(s + 1) * LANES]
        if rope and s < n_rot:
            ys = ys * cos + pltpu.roll(ys, LANES - 16, 1) * sa + pltpu.roll(ys, 16, 1) * sb
        if s * LANES < q_cols:
            ys = ys * scale
        o_ref[:, s * LANES:(s + 1) * LANES] = ys.astype(o_ref.dtype)


def _qkv_proj(h, g, mod, w, tables, *, seq, tm, q_cols, k_cols):
    t, d = h.shape
    n = w.shape[1]
    tiles_per_seq = seq // tm
    bm = mod.shape[0]
    mod_idx = (lambda i: (i // tiles_per_seq, 0, 0)) if bm > 1 else (lambda i: (0, 0, 0))
    in_specs = [
        pl.BlockSpec((tm, d), lambda i: (i, 0)),
        pl.BlockSpec((1, d), lambda i: (0, 0)),
        pl.BlockSpec((None, 2, d), mod_idx),
        pl.BlockSpec((d, n), lambda i: (0, 0)),
    ]
    args = [h, g, mod, w]
    rope = tables is not None
    if rope:
        for tb in tables:
            in_specs.append(pl.BlockSpec((tm, LANES), lambda i: (i % tiles_per_seq, 0)))
            args.append(tb)
    return pl.pallas_call(
        functools.partial(_qkv_kernel, rope=rope, n_rot=(q_cols + k_cols) // LANES, q_cols=q_cols,
                          scale=ATTN_HEAD_DIM ** -0.5),
        out_shape=jax.ShapeDtypeStruct((t, n), BF16),
        grid=(t // tm,),
        in_specs=in_specs,
        out_specs=pl.BlockSpec((tm, n), lambda i: (i, 0)),
        compiler_params=_cparams("parallel"),
        name="qkv_proj",
    )(*args)


def _outproj_kernel(a_ref, w_ref, h_ref, mod_ref, o_ref):
    y = jnp.dot(a_ref[...], w_ref[...], preferred_element_type=F32)
    o_ref[...] = h_ref[...] + mod_ref[...] * y


def _outproj(a, w, h, gate, *, seq, tm):
    t, k = a.shape
    d = w.shape[1]
    tiles_per_seq = seq // tm
    bm = gate.shape[0]
    mod_idx = (lambda i: (i // tiles_per_seq, 0, 0)) if bm > 1 else (lambda i: (0, 0, 0))
    return pl.pallas_call(
        _outproj_kernel,
        out_shape=jax.ShapeDtypeStruct((t, d), F32),
        grid=(t // tm,),
        in_specs=[
            pl.BlockSpec((tm, k), lambda i: (i, 0)),
            pl.BlockSpec((k, d), lambda i: (0, 0)),
            pl.BlockSpec((tm, d), lambda i: (i, 0)),
            pl.BlockSpec((None, 1, d), mod_idx),
        ],
        out_specs=pl.BlockSpec((tm, d), lambda i: (i, 0)),
        compiler_params=_cparams("parallel"),
        name="outproj",
    )(a, w, h, gate)


def _ffn_kernel(x_ref, g_ref, mod_ref, win_ref, wo_ref, *refs, final, hidden, chunk):
    if final:
        fg_ref, o_ref = refs
    else:
        (o_ref,) = refs
    x = x_ref[...]
    a = _norm_mod(x, g_ref[...], mod_ref[0:1, :], mod_ref[1:2, :]).astype(BF16)
    dot = functools.partial(jnp.dot, preferred_element_type=F32)
    acc = None
    for c0 in range(0, hidden, chunk):
        gte = dot(a, win_ref[:, c0:c0 + chunk])
        up = dot(a, win_ref[:, hidden + c0:hidden + c0 + chunk])
        part = dot((_silu(gte) * up).astype(BF16), wo_ref[c0:c0 + chunk, :])
        acc = part if acc is None else acc + part
    y = x + mod_ref[2:3, :] * acc
    if final:
        y = y * lax.rsqrt(jnp.mean(y * y, axis=-1, keepdims=True) + RMS_EPS) * fg_ref[...]
    o_ref[...] = y


def _ffn(h, g, mod, w_in, w_out, *, seq, tm, chunk, final_g=None):
    t, d = h.shape
    hidden = w_out.shape[0]
    tiles_per_seq = seq // tm
    bm = mod.shape[0]
    mod_idx = (lambda i: (i // tiles_per_seq, 0, 0)) if bm > 1 else (lambda i: (0, 0, 0))
    resident = pl.Buffered(1)
    in_specs = [
        pl.BlockSpec((tm, d), lambda i: (i, 0)),
        pl.BlockSpec((1, d), lambda i: (0, 0)),
        pl.BlockSpec((None, 3, d), mod_idx),
        pl.BlockSpec((d, 2 * hidden), lambda i: (0, 0), pipeline_mode=resident),
        pl.BlockSpec((hidden, d), lambda i: (0, 0), pipeline_mode=resident),
    ]
    args = [h, g, mod, w_in, w_out]
    if final_g is not None:
        in_specs.append(pl.BlockSpec((1, d), lambda i: (0, 0)))
        args.append(final_g)
    return pl.pallas_call(
        functools.partial(_ffn_kernel, final=final_g is not None, hidden=hidden, chunk=chunk),
        out_shape=jax.ShapeDtypeStruct((t, d), F32),
        grid=(t // tm,),
        in_specs=in_specs,
        out_specs=pl.BlockSpec((tm, d), lambda i: (i, 0)),
        compiler_params=_cparams("parallel"),
        name="ffn",
    )(*args)


HALO = 16


def _ssd_inproj_kernel(x_ref, xp_ref, xn_ref, g_ref, mod_ref, w_ref, wdt_ref, cw_ref, cb_ref,
                       z_ref, xc_ref, dt_ref, a_scr, *, tiles_per_seq, tm, d_inner, chunk):
    pos = pl.program_id(0) % tiles_per_seq
    g, shift, scale = g_ref[...], mod_ref[0:1, :], mod_ref[1:2, :]
    a_prev = jnp.where(pos == 0, 0.0, _norm_mod(xp_ref[...], g, shift, scale))
    a_next = jnp.where(pos == tiles_per_seq - 1, 0.0, _norm_mod(xn_ref[...], g, shift, scale))
    a_scr[0:HALO, :] = a_prev.astype(BF16)
    a_scr[HALO:HALO + tm, :] = _norm_mod(x_ref[...], g, shift, scale).astype(BF16)
    a_scr[HALO + tm:, :] = a_next.astype(BF16)
    dot = functools.partial(jnp.dot, preferred_element_type=F32)
    a_main = a_scr[HALO:HALO + tm, :]
    dt_ref[...] = dot(a_main, wdt_ref[...])

    rows = tm + 2 * HALO
    half = SSD_CONV_W // 2
    spc = chunk // LANES
    t0, nt = HALO // SUBLANES, tm // SUBLANES
    sub = lax.broadcasted_iota(jnp.int32, (1, SUBLANES, LANES), 1)
    for c0 in range(0, d_inner, chunk):
        z_ref[:, c0:c0 + chunk] = dot(a_main, w_ref[:, c0:c0 + chunk])
    for c in range(cw_ref.shape[0] // spc):
        pre = dot(a_scr[...], w_ref[:, d_inner + c * chunk:d_inner + (c + 1) * chunk])
        for s in range(spc):
            e = pre[:, s * LANES:(s + 1) * LANES].reshape(rows // SUBLANES, SUBLANES, LANES)
            cw = cw_ref[c * spc + s]
            acc = cb_ref[c * spc + s] + cw[half:half + 1, :] * e[t0:t0 + nt]
            rot = {1: pltpu.roll(e, 1, 1), 2: pltpu.roll(e, 2, 1)}
            rot[6] = pltpu.roll(rot[2], 4, 1)
            rot[7] = pltpu.roll(rot[6], 1, 1)
            for k in range(SSD_CONV_W):
                sh = half - k
                if sh == 0:
                    continue
                r = rot[sh % SUBLANES]
                if sh > 0:
                    tap = jnp.where(sub < sh, r[t0 - 1:t0 - 1 + nt], r[t0:t0 + nt])
                else:
                    tap = jnp.where(sub < SUBLANES + sh, r[t0:t0 + nt], r[t0 + 1:t0 + 1 + nt])
                acc = acc + cw[k:k + 1, :] * tap
            xc_ref[c * spc + s] = _silu(acc).reshape(tm, LANES)


def _ssd_inproj(h, g, mod, w_main, w_dt, conv_w, conv_b, *, seq, tm, d_inner, chunk):
    t, d = h.shape
    slabs = conv_w.shape[0]
    tiles_per_seq = seq // tm
    bm = mod.shape[0]
    mod_idx = (lambda i: (i // tiles_per_seq, 0, 0)) if bm > 1 else (lambda i: (0, 0, 0))
    rb = tm // HALO
    last_rb = t // HALO - 1
    resident = pl.Buffered(1)
    return pl.pallas_call(
        functools.partial(_ssd_inproj_kernel, tiles_per_seq=tiles_per_seq, tm=tm, d_inner=d_inner, chunk=chunk),
        out_shape=[jax.ShapeDtypeStruct((t, d_inner), F32),
                   jax.ShapeDtypeStruct((slabs, t, LANES), F32),
                   jax.ShapeDtypeStruct((t, w_dt.shape[1]), F32)],
        grid=(t // tm,),
        in_specs=[
            pl.BlockSpec((tm, d), lambda i: (i, 0)),
            pl.BlockSpec((HALO, d), lambda i: (jnp.maximum(i * rb - 1, 0), 0)),
            pl.BlockSpec((HALO, d), lambda i: (jnp.minimum((i + 1) * rb, last_rb), 0)),
            pl.BlockSpec((1, d), lambda i: (0, 0)),
            pl.BlockSpec((None, 2, d), mod_idx),
            pl.BlockSpec(w_main.shape, lambda i: (0, 0), pipeline_mode=resident),
            pl.BlockSpec(w_dt.shape, lambda i: (0, 0), pipeline_mode=resident),
            pl.BlockSpec(conv_w.shape, lambda i: (0, 0, 0)),
            pl.BlockSpec(conv_b.shape, lambda i: (0, 0, 0)),
        ],
        out_specs=[pl.BlockSpec((tm, d_inner), lambda i: (i, 0)),
                   pl.BlockSpec((slabs, tm, LANES), lambda i: (0, i, 0)),
                   pl.BlockSpec((tm, w_dt.shape[1]), lambda i: (i, 0))],
        scratch_shapes=[pltpu.VMEM((tm + 2 * HALO, d), BF16)],
        compiler_params=_cparams("parallel"),
        name="ssd_inproj",
    )(h, h, h, g, mod, w_main, w_dt, conv_w, conv_b)


def _ssd_kernel(x_ref, b_ref, c_ref, dt_ref, bias_ref, alog_ref, s0_ref, *refs,
                reverse, finish, heads, col_off):
    if finish:
        yf_ref, z_ref, d_ref, nw_ref, o_ref, sfin_ref, state, y_scr = refs
    else:
        o_ref, sfin_ref, state = refs
        y_scr = o_ref
    L = SSD_CHUNK
    hd = SSD_HEADDIM
    hpg = heads // SSD_GROUPS
    gw = hpg * hd
    pairs = gw // LANES
    j = pl.program_id(1)

    @pl.when(j == 0)
    def _():
        state[...] = s0_ref[...]

    row = lax.broadcasted_iota(jnp.int32, (L, L), 0)
    col = lax.broadcasted_iota(jnp.int32, (L, L), 1)
    mask = (col >= row) if reverse else (col <= row)
    tri = jnp.where(mask, 1.0, 0.0).astype(BF16)
    first = col < hd
    first_row = lax.broadcasted_iota(jnp.int32, (1, LANES), 1) < hd

    v = dt_ref[...] + bias_ref[...]
    dt = jnp.maximum(v, 0.0) + jnp.log1p(jnp.exp(-jnp.abs(v)))
    dta = dt * (-jnp.exp(alog_ref[...]))
    p1, p2, p3 = _split3(dta)
    dot = functools.partial(jnp.dot, preferred_element_type=F32)
    dotm = lambda a, b: lax.dot_general(a, b, (((1,), (0,)), ((), ())), preferred_element_type=F32)
    a_cum = dot(tri, p1) + (dot(tri, p2) + dot(tri, p3))
    b_log = a_cum - jnp.log(dt)
    a_tot = a_cum[0:1, :] if reverse else a_cum[L - 1:L, :]
    a_cum2 = a_cum * LOG2E
    b_t = (b_log * LOG2E).T
    dtw_t = jnp.exp(a_tot - b_log).T
    e_tot = jnp.exp(a_tot)

    for g in range(SSD_GROUPS):
        bg = b_ref[g]
        cg = c_ref[g].astype(BF16)
        cb = lax.dot_general(cg, bg.astype(BF16), (((1,), (1,)), ((), ())), preferred_element_type=F32)
        bg_t = bg.T
        y_in = dot(cg, state[:, g * gw:(g + 1) * gw].astype(BF16))
        for k in range(pairs):
            ca = col_off + g * hpg + 2 * k
            lo = g * gw + k * LANES
            xp = x_ref[g * pairs + k]
            x_ab = jnp.concatenate([jnp.where(first, xp, 0.0), jnp.where(first, 0.0, xp)], axis=0).astype(BF16)
            a_a = jnp.broadcast_to(a_cum2[:, ca:ca + 1], (L, L))
            a_b = jnp.broadcast_to(a_cum2[:, ca + 1:ca + 2], (L, L))
            m_a = cb * jnp.exp2(jnp.where(mask, a_a - b_t[ca:ca + 1, :], -jnp.inf))
            m_b = cb * jnp.exp2(jnp.where(mask, a_b - b_t[ca + 1:ca + 2, :], -jnp.inf))
            e_cum = jnp.exp2(jnp.where(first, a_a, a_b))
            y_pair = dotm(jnp.concatenate([m_a, m_b], axis=1), x_ab) + y_in[:, k * LANES:(k + 1) * LANES] * e_cum
            if finish:
                y_pair = y_pair + xp * d_ref[:, lo:lo + LANES]
            y_scr[:, lo:lo + LANES] = y_pair
            bt_a = bg_t * dtw_t[ca:ca + 1, :]
            bt_b = bg_t * dtw_t[ca + 1:ca + 2, :]
            e_end = jnp.where(first_row, e_tot[:, ca:ca + 1], e_tot[:, ca + 1:ca + 2])
            state[:, lo:lo + LANES] = (state[:, lo:lo + LANES] * e_end
                                       + dotm(jnp.concatenate([bt_a, bt_b], axis=1), x_ab))

    if finish:
        y = y_scr[...] + yf_ref[...]
        gated = y * _silu(z_ref[...])
        gn = gated * lax.rsqrt(jnp.mean(gated * gated, axis=-1, keepdims=True) + RMS_EPS) * nw_ref[...]
        o_ref[...] = gn.astype(o_ref.dtype)

    @pl.when(j == pl.num_programs(1) - 1)
    def _():
        sfin_ref[...] = state[...]


def _ssd_scan(xc, dt_raw, dt_bias, a_log, state0, *, batch, seq, reverse, d_inner, finish_args=None):
    t = xc.shape[1]
    L = SSD_CHUNK
    nc = seq // L
    heads = d_inner // SSD_HEADDIM
    x_slabs = d_inner // LANES
    gs = SSD_GROUPS * SSD_STATE // LANES
    assert SSD_STATE == LANES and x_slabs % gs == 0
    finish = finish_args is not None

    def rows(b, j):
        return b * nc + ((nc - 1 - j) if reverse else j)

    in_specs = [
        pl.BlockSpec((x_slabs, L, LANES), lambda b, j: (0, rows(b, j), 0)),
        pl.BlockSpec((gs, L, LANES), lambda b, j: (x_slabs // gs, rows(b, j), 0)),
        pl.BlockSpec((gs, L, LANES), lambda b, j: (x_slabs // gs + 1, rows(b, j), 0)),
        pl.BlockSpec((L, LANES), lambda b, j: (rows(b, j), 0)),
        pl.BlockSpec((1, LANES), lambda b, j: (0, 0)),
        pl.BlockSpec((1, LANES), lambda b, j: (0, 0)),
        pl.BlockSpec((None, SSD_STATE, d_inner), lambda b, j: (b, 0, 0)),
    ]
    args = [xc, xc, xc, dt_raw, dt_bias, a_log, state0]
    if finish:
        y_f, z, d_exp, norm_w = finish_args
        in_specs += [
            pl.BlockSpec((L, d_inner), lambda b, j: (rows(b, j), 0)),
            pl.BlockSpec((L, d_inner), lambda b, j: (rows(b, j), 0)),
            pl.BlockSpec((1, d_inner), lambda b, j: (0, 0)),
            pl.BlockSpec((1, d_inner), lambda b, j: (0, 0)),
        ]
        args += [y_f, z, d_exp, norm_w]
    out_dtype = BF16 if finish else F32
    return pl.pallas_call(
        functools.partial(_ssd_kernel, reverse=reverse, finish=finish, heads=heads,
                          col_off=heads if reverse else 0),
        out_shape=[jax.ShapeDtypeStruct((t, d_inner), out_dtype),
                   jax.ShapeDtypeStruct((batch, SSD_STATE, d_inner), F32)],
        grid=(batch, nc),
        in_specs=in_specs,
        out_specs=[pl.BlockSpec((L, d_inner), lambda b, j: (rows(b, j), 0)),
                   pl.BlockSpec((None, SSD_STATE, d_inner), lambda b, j: (b, 0, 0))],
        scratch_shapes=[pltpu.VMEM((SSD_STATE, d_inner), F32)]
        + ([pltpu.VMEM((L, d_inner), F32)] if finish else []),
        compiler_params=_cparams("parallel", "arbitrary"),
        name="ssd_scan_bwd" if reverse else "ssd_scan_fwd",
    )(*args)


def _attn_kernel(sink_ref, q_ref, kc_ref, vc_ref, wo_ref, h_ref, gate_ref, *refs, band, q_heads):
    if band:
        kp_ref, k0_ref, kn_ref, vp_ref, v0_ref, vn_ref, o_ref = refs
    else:
        (o_ref,) = refs
    hd = ATTN_HEAD_DIM
    group = q_heads // ATTN_KV_HEADS
    blk = q_ref.shape[0]
    dot = functools.partial(jnp.dot, preferred_element_type=F32)
    dot_t = lambda a, b: lax.dot_general(a, b, (((1,), (1,)), ((), ())), preferred_element_type=F32)
    gq = group * blk
    key_idx = lax.broadcasted_iota(jnp.int32, (blk, gq), 0)
    qry_idx = lax.broadcasted_iota(jnp.int32, (blk, gq), 1) % blk
    lane_head = lax.broadcasted_iota(jnp.int32, (1, gq), 1) // blk
    segs = [(kc_ref[c0:c0 + blk, :], vc_ref[c0:c0 + blk, :], None) for c0 in range(0, kc_ref.shape[0], blk)]
    if band:
        j = pl.program_id(1)
        ok_prev = key_idx >= qry_idx + jnp.where(j > 0, 0, 2 * blk)
        ok_next = key_idx + jnp.where(j < pl.num_programs(1) - 1, 0, 2 * blk) <= qry_idx
        segs += [(kp_ref[...], vp_ref[...], ok_prev), (k0_ref[...], v0_ref[...], None),
                 (kn_ref[...], vn_ref[...], ok_next)]
    v_t = [v.astype(F32).T.astype(BF16) for _, v, _ in segs]
    o_rows = []
    for kv in range(ATTN_KV_HEADS):
        ksl = slice(kv * hd, (kv + 1) * hd)
        h0 = kv * group
        qs = jnp.concatenate([q_ref[:, (h0 + i) * hd:(h0 + i + 1) * hd] for i in range(group)], axis=0)
        sink = jnp.full((1, gq), sink_ref[h0], F32)
        for i in range(1, group):
            sink = jnp.where(lane_head == i, sink_ref[h0 + i], sink)
        s_tiles = []
        for k, _, ok in segs:
            s = dot_t(k[:, ksl], qs)
            s_tiles.append(s if ok is None else jnp.where(ok, s, -jnp.inf))
        m = jnp.maximum(jnp.max(functools.reduce(jnp.maximum, s_tiles), axis=0, keepdims=True), sink)
        p_tiles = [jnp.exp(s - m) for s in s_tiles]
        den = jnp.exp(sink - m) + jnp.sum(functools.reduce(jnp.add, p_tiles), axis=0, keepdims=True)
        o_t = functools.reduce(jnp.add, [dot(vt[ksl, :], p.astype(BF16)) for vt, p in zip(v_t, p_tiles)])
        o_t = o_t * (1.0 / den)
        o_rows += [o_t[:, i * blk:(i + 1) * blk] for i in range(group)]
    o = jnp.concatenate(o_rows, axis=0).T.astype(BF16)
    o_ref[...] = h_ref[...] + gate_ref[...] * dot(o, wo_ref[...])


def _attention(sink, qkv, qkv_ctx, w_o, h, gate, *, batch, seq, n_ctx, q_heads, band):
    blk = ATTN_BLOCK
    assert n_ctx % blk == 0 and seq % blk == 0
    nb = seq // blk
    d = h.shape[1]
    qd = q_heads * ATTN_HEAD_DIM
    kd = ATTN_KV_HEADS * ATTN_HEAD_DIM
    kcol, vcol = qd // kd, qd // kd + 1
    gate_idx = (lambda b, j: (b, 0, 0)) if gate.shape[0] > 1 else (lambda b, j: (0, 0, 0))
    in_specs = [
        pl.BlockSpec(memory_space=pltpu.SMEM),
        pl.BlockSpec((blk, qd), lambda b, j: (b * nb + j, 0)),
        pl.BlockSpec((n_ctx, kd), lambda b, j: (b, kcol)),
        pl.BlockSpec((n_ctx, kd), lambda b, j: (b, vcol)),
        pl.BlockSpec(w_o.shape, lambda b, j: (0, 0), pipeline_mode=pl.Buffered(1)),
        pl.BlockSpec((blk, d), lambda b, j: (b * nb + j, 0)),
        pl.BlockSpec((None, 1, d), gate_idx),
    ]
    args = [sink, qkv, qkv_ctx, qkv_ctx, w_o, h, gate]
    if band:
        prev = lambda b, j: b * nb + jnp.maximum(j - 1, 0)
        cur = lambda b, j: b * nb + j
        nxt = lambda b, j: b * nb + jnp.minimum(j + 1, nb - 1)
        for colb in (kcol, vcol):
            for f in (prev, cur, nxt):
                in_specs.append(pl.BlockSpec((blk, kd), functools.partial(lambda b, j, f, colb: (f(b, j), colb),
                                                                          f=f, colb=colb)))
                args.append(qkv)
    return pl.pallas_call(
        functools.partial(_attn_kernel, band=band, q_heads=q_heads),
        out_shape=jax.ShapeDtypeStruct((batch * seq, d), F32),
        grid=(batch, nb),
        in_specs=in_specs,
        out_specs=pl.BlockSpec((blk, d), lambda b, j: (b * nb + j, 0)),
        compiler_params=_cparams("parallel", "parallel"),
        name="window_attn" if band else "ctx_attn",
    )(*args)


def _gmlp_kernel(x_ref, g_ref, mod_ref, win_ref, lg_ref, lb_ref, ws_ref, bs_ref, wo_ref, o_ref, gm_scr, *, chunk):
    x = x_ref[...]
    a = _norm_mod(x, g_ref[...], mod_ref[0:1, :], mod_ref[1:2, :]).astype(BF16)
    width = wo_ref.shape[0]
    dot = functools.partial(jnp.dot, preferred_element_type=F32)

    def gelu_proj(c0):
        y = dot(a, win_ref[:, c0:c0 + chunk])
        return 0.5 * y * (1.0 + lax.erf(y * (1.0 / math.sqrt(2.0))))

    v = jnp.concatenate([gelu_proj(width + c0) for c0 in range(0, width, chunk)], axis=1)
    mu = jnp.mean(v, axis=-1, keepdims=True)
    vc = v - mu
    var = jnp.mean(vc * vc, axis=-1, keepdims=True)
    vn = (vc * lax.rsqrt(var + LN_EPS) * lg_ref[...] + lb_ref[...]).astype(BF16)
    gd = width // GMLP_GROUPS
    ch = GMLP_CHUNK
    for c0 in range(0, width, chunk):
        u = gelu_proj(c0)
        for g in range(c0 // gd, (c0 + chunk) // gd):
            for r in range(0, x.shape[0], ch):
                sv = dot(ws_ref[g], vn[r:r + ch, g * gd:(g + 1) * gd]) + bs_ref[:, g:g + 1]
                gm_scr[r:r + ch, g * gd:(g + 1) * gd] = (u[r:r + ch, g * gd - c0:(g + 1) * gd - c0] * sv).astype(BF16)
    o_ref[...] = x + mod_ref[2:3, :] * dot(gm_scr[...], wo_ref[...])


def _gmlp(h, g, mod, w_in, ln_g, ln_b, w_s, b_s_t, w_out, *, seq, tm, chunk):
    t, d = h.shape
    width = w_out.shape[0]
    assert tm % GMLP_CHUNK == 0 and chunk % (width // GMLP_GROUPS) == 0
    tiles_per_seq = seq // tm
    bm = mod.shape[0]
    mod_idx = (lambda i: (i // tiles_per_seq, 0, 0)) if bm > 1 else (lambda i: (0, 0, 0))
    resident = pl.Buffered(1)
    return pl.pallas_call(
        functools.partial(_gmlp_kernel, chunk=chunk),
        out_shape=jax.ShapeDtypeStruct((t, d), F32),
        grid=(t // tm,),
        in_specs=[
            pl.BlockSpec((tm, d), lambda i: (i, 0)),
            pl.BlockSpec((1, d), lambda i: (0, 0)),
            pl.BlockSpec((None, 3, d), mod_idx),
            pl.BlockSpec(w_in.shape, lambda i: (0, 0), pipeline_mode=resident),
            pl.BlockSpec((1, width), lambda i: (0, 0)),
            pl.BlockSpec((1, width), lambda i: (0, 0)),
            pl.BlockSpec(w_s.shape, lambda i: (0, 0, 0)),
            pl.BlockSpec(b_s_t.shape, lambda i: (0, 0)),
            pl.BlockSpec(w_out.shape, lambda i: (0, 0), pipeline_mode=resident),
        ],
        out_specs=pl.BlockSpec((tm, d), lambda i: (i, 0)),
        scratch_shapes=[pltpu.VMEM((tm, width), BF16)],
        compiler_params=_cparams("parallel"),
        name="gmlp",
    )(h, g, mod, w_in, ln_g, ln_b, w_s, b_s_t, w_out)


def _rope_tables(n):
    freqs = ATTN_HEAD_DIM // 4
    rows = n // GRID_W
    row = jnp.repeat(jnp.arange(rows, dtype=jnp.int32), GRID_W, total_repeat_length=n)
    col = jnp.tile(jnp.arange(GRID_W, dtype=jnp.int32), rows)
    inv = ROPE_BASE ** (-jnp.arange(freqs, dtype=F32) / freqs)
    ang = jnp.stack([row.astype(F32)[:, None] * inv, col.astype(F32)[:, None] * inv], axis=1)
    ang = jnp.repeat(ang[:, :, None, :], 2, axis=2).reshape(n, ATTN_HEAD_DIM)
    cos, sin = jnp.cos(ang), jnp.sin(ang)
    reps = LANES // ATTN_HEAD_DIM
    cos, sin = jnp.tile(cos, (1, reps)), jnp.tile(sin, (1, reps))
    first_half = jnp.asarray((np.arange(LANES) // freqs) % 2 == 0)[None, :]
    return cos, jnp.where(first_half, -sin, 0.0), jnp.where(first_half, 0.0, sin)


def _ssd_mixer(streams, norm_g, w_in, conv_w, conv_b, a_log, dt_bias, d_skip, norm_w, w_out, need_ctx_out):
    d_inner = norm_w.shape[0]
    conv_dim = conv_w.shape[1]
    heads = d_inner // SSD_HEADDIM
    w_main = w_in[:, :d_inner + conv_dim].astype(BF16)
    w_dt = jnp.pad(w_in[:, d_inner + conv_dim:], ((0, 0), (0, LANES - 2 * heads))).astype(BF16)
    bias = jnp.pad(dt_bias.reshape(1, 2 * heads), ((0, 0), (0, LANES - 2 * heads)))
    alog = jnp.pad(a_log.reshape(1, 2 * heads), ((0, 0), (0, LANES - 2 * heads)))
    d_exp = jnp.repeat(d_skip, SSD_HEADDIM).reshape(1, d_inner)
    nw = norm_w.reshape(1, d_inner)
    w_out = w_out.astype(BF16)
    slabs = conv_dim // LANES
    conv_w = jnp.pad(conv_w, ((0, SUBLANES - SSD_CONV_W), (0, 0))).reshape(SUBLANES, slabs, LANES).transpose(1, 0, 2)
    conv_b = conv_b.reshape(slabs, 1, LANES)

    outs = []
    s_f = s_b = None
    for name, s in streams:
        if s_f is None:
            s_f = s_b = jnp.zeros((s["batch"], SSD_STATE, d_inner), F32)
        z, xc, dt_raw = _ssd_inproj(s["h"], norm_g, s["mod"][:, 0:2], w_main, w_dt, conv_w, conv_b,
                                    seq=s["seq"], tm=s["tm"], d_inner=d_inner, chunk=1024)
        y_f, s_f = _ssd_scan(xc, dt_raw, bias, alog, s_f, batch=s["batch"], seq=s["seq"], reverse=False,
                             d_inner=d_inner)
        gn, s_b = _ssd_scan(xc, dt_raw, bias, alog, s_b, batch=s["batch"], seq=s["seq"], reverse=True,
                            d_inner=d_inner, finish_args=(y_f, z, d_exp, nw))
        if name == "ctx" and not need_ctx_out:
            outs.append(None)
        else:
            tm_out = 2 * s["tm"] if s["seq"] % (2 * s["tm"]) == 0 else s["tm"]
            outs.append(_outproj(gn, w_out, s["h"], s["mod"][:, 2:3], seq=s["seq"], tm=tm_out))
    return outs


def _attn_mixer(streams, norm_g, w_qkv, sink, w_o, tables, need_ctx_out):
    (_, sc), (_, sl) = streams
    q_heads = sink.shape[0]
    qd = q_heads * ATTN_HEAD_DIM
    kd = ATTN_KV_HEADS * ATTN_HEAD_DIM
    w_qkv = w_qkv.astype(BF16)
    w_o = w_o.astype(BF16)
    qkv_c = _qkv_proj(sc["h"], norm_g, sc["mod"][:, 0:2], w_qkv, None, seq=sc["seq"], tm=sc["tm"],
                      q_cols=qd, k_cols=kd)
    qkv_l = _qkv_proj(sl["h"], norm_g, sl["mod"][:, 0:2], w_qkv, tables, seq=sl["seq"], tm=sl["tm"],
                      q_cols=qd, k_cols=kd)
    h_l = _attention(sink, qkv_l, qkv_c, w_o, sl["h"], sl["mod"][:, 2:3], batch=sl["batch"], seq=sl["seq"],
                     n_ctx=sc["seq"], q_heads=q_heads, band=True)
    h_c = None
    if need_ctx_out:
        h_c = _attention(sink, qkv_c, qkv_c, w_o, sc["h"], sc["mod"][:, 2:3], batch=sc["batch"], seq=sc["seq"],
                         n_ctx=sc["seq"], q_heads=q_heads, band=False)
    return [h_c, h_l]


def _gmlp_mixer(streams, norm_g, w_in, ln_g, ln_b, w_s, b_s, w_out, need_ctx_out):
    width = ln_g.shape[0]
    w_in = w_in.astype(BF16)
    w_out = w_out.astype(BF16)
    w_s = w_s.astype(BF16)
    outs = []
    for name, s in streams:
        if name == "ctx" and not need_ctx_out:
            outs.append(None)
            continue
        outs.append(_gmlp(s["h"], norm_g, s["mod"][:, 0:3], w_in, ln_g.reshape(1, width), ln_b.reshape(1, width),
                          w_s, b_s.T, w_out, seq=s["seq"], tm=s["tm"], chunk=1024))
    return outs


def kernel(x, c, ctx, c_ctx, w_mod, b_mod, norm_g, final_g, ssd_w_in, ssd_conv_w, ssd_conv_b, ssd_a_log, ssd_dt_bias, ssd_d, ssd_norm_w, ssd_w_out, attn_w_qkv, attn_sink, attn_w_o, gmlp_w_in, gmlp_ln_g, gmlp_ln_b, gmlp_w_s, gmlp_b_s, gmlp_w_out, ffn_w_in, ffn_w_out):
    batch, seq, d = x.shape
    n_ctx = ctx.shape[1]
    depth = w_mod.shape[0]
    hidden = ffn_w_out.shape[1]

    pad_rows = -(batch + 1) % 16
    cc = jnp.concatenate([c, c_ctx[None, :], jnp.zeros((pad_rows, d), F32)], axis=0)
    mods = _modulation(cc, w_mod, b_mod)
    tables = _rope_tables(seq)

    h_lat = x.reshape(batch * seq, d)
    h_ctx = ctx.reshape(batch * n_ctx, d)
    tm_lat = 512 if seq % 512 == 0 else 256
    tm_ctx = 256
    ffn_chunk = hidden

    for i in range(depth):
        last = i == depth - 1
        kind, j = i % 3, i // 3
        mod_lat = mods[i, :batch].reshape(batch, N_MOD, d)
        mod_ctx = mods[i, batch:batch + 1].reshape(1, N_MOD, d)
        g1 = norm_g[i, 0].reshape(1, d)
        g2 = norm_g[i, 1].reshape(1, d)
        streams = [
            ("ctx", dict(h=h_ctx, mod=mod_ctx, batch=batch, seq=n_ctx, tm=tm_ctx)),
            ("lat", dict(h=h_lat, mod=mod_lat, batch=batch, seq=seq, tm=tm_lat)),
        ]
        need_ctx_out = not last
        if kind == 0:
            h_ctx_new, h_lat = _ssd_mixer(streams, g1, ssd_w_in[j], ssd_conv_w[j], ssd_conv_b[j], ssd_a_log[j],
                                          ssd_dt_bias[j], ssd_d[j], ssd_norm_w[j], ssd_w_out[j], need_ctx_out)
        elif kind == 1:
            h_ctx_new, h_lat = _attn_mixer(streams, g1, attn_w_qkv[j], attn_sink[j], attn_w_o[j], tables,
                                           need_ctx_out)
        else:
            h_ctx_new, h_lat = _gmlp_mixer(streams, g1, gmlp_w_in[j], gmlp_ln_g[j], gmlp_ln_b[j], gmlp_w_s[j],
                                           gmlp_b_s[j], gmlp_w_out[j], need_ctx_out)

        w_in = ffn_w_in[i].astype(BF16)
        w_out = ffn_w_out[i].astype(BF16)
        h_lat = _ffn(h_lat, g2, mod_lat[:, 3:6], w_in, w_out, seq=seq, tm=tm_lat, chunk=ffn_chunk,
                     final_g=final_g.reshape(1, d) if last else None)
        if need_ctx_out:
            h_ctx = _ffn(h_ctx_new, g2, mod_ctx[:, 3:6], w_in, w_out, seq=n_ctx, tm=tm_ctx, chunk=ffn_chunk)
    return h_lat.reshape(batch, seq, d)
```

```python
import functools
import math

import jax
import jax.numpy as jnp
import numpy as np
from jax import lax
from jax.experimental import pallas as pl
from jax.experimental.pallas import tpu as pltpu

F32 = jnp.float32
BF16 = jnp.bfloat16

N_MOD = 6
GRID_W = 64
SSD_HEADDIM = 64
SSD_GROUPS = 4
SSD_STATE = 128
SSD_CONV_W = 5
SSD_CHUNK = 128
ATTN_HEAD_DIM = 64
ATTN_KV_HEADS = 4
ATTN_WINDOW = 128
ATTN_BLOCK = 128
ROPE_BASE = 10000.0
GMLP_GROUPS = 8
GMLP_CHUNK = 128
RMS_EPS = 1e-6
LOG2E = math.log2(math.e)
LN_EPS = 1e-5

LANES = 128
SUBLANES = 8
VMEM_LIMIT = 56 * 1024 * 1024


def _cparams(*sem):
    return pltpu.CompilerParams(dimension_semantics=sem, vmem_limit_bytes=VMEM_LIMIT)


def _silu(x):
    return x * (1.0 / (1.0 + jnp.exp2(x * -LOG2E)))


def _norm_mod(x, g, shift, scale):
    y = x * lax.rsqrt(jnp.mean(x * x, axis=-1, keepdims=True) + RMS_EPS)
    return (y * g) * (1.0 + scale) + shift


def _split3(x):
    hi = x.astype(BF16)
    r1 = x - hi.astype(F32)
    mid = r1.astype(BF16)
    lo = (r1 - mid.astype(F32)).astype(BF16)
    return hi, mid, lo


def _mod_kernel(c_ref, w_ref, b_ref, o_ref):
    x = _silu(c_ref[...])
    w = w_ref[...]
    xh = x.astype(BF16)
    xl = (x - xh.astype(F32)).astype(BF16)
    wh = w.astype(BF16)
    wl = (w - wh.astype(F32)).astype(BF16)
    dot = functools.partial(jnp.dot, preferred_element_type=F32)
    o_ref[...] = dot(xh, wh) + (dot(xh, wl) + dot(xl, wh)) + b_ref[...]


def _modulation(cc, w_mod, b_mod):
    depth, d, n = w_mod.shape
    rows = cc.shape[0]
    tn = 1536
    return pl.pallas_call(
        _mod_kernel,
        out_shape=jax.ShapeDtypeStruct((depth, rows, n), F32),
        grid=(depth, n // tn),
        in_specs=[
            pl.BlockSpec((rows, d), lambda l, j: (0, 0)),
            pl.BlockSpec((None, d, tn), lambda l, j: (l, 0, j)),
            pl.BlockSpec((None, 1, tn), lambda l, j: (l, 0, j)),
        ],
        out_specs=pl.BlockSpec((None, rows, tn), lambda l, j: (l, 0, j)),
        compiler_params=_cparams("parallel", "parallel"),
        name="modulation",
    )(cc, w_mod, b_mod.reshape(depth, 1, n))


def _qkv_kernel(x_ref, g_ref, mod_ref, w_ref, *refs, rope, n_rot, q_cols, scale):
    if rope:
        cos_ref, sa_ref, sb_ref, o_ref = refs
    else:
        (o_ref,) = refs
    a = _norm_mod(x_ref[...], g_ref[...], mod_ref[0:1, :], mod_ref[1:2, :]).astype(BF16)
    y = jnp.dot(a, w_ref[...], preferred_element_type=F32)
    n = y.shape[1]
    if rope:
        cos, sa, sb = cos_ref[...], sa_ref[...], sb_ref[...]
    for s in range(n // LANES):
        ys = y[:, s * LANES:(s + 1) * LANES]
        if rope and s < n_rot:
            ys = ys * cos + pltpu.roll(ys, LANES - 16, 1) * sa + pltpu.roll(ys, 16, 1) * sb
        if s * LANES < q_cols:
            ys = ys * scale
        o_ref[:, s * LANES:(s + 1) * LANES] = ys.astype(o_ref.dtype)


def _qkv_proj(h, g, mod, w, tables, *, seq, tm, q_cols, k_cols):
    t, d = h.shape
    n = w.shape[1]
    tiles_per_seq = seq // tm
    bm = mod.shape[0]
    mod_idx = (lambda i: (i // tiles_per_seq, 0, 0)) if bm > 1 else (lambda i: (0, 0, 0))
    in_specs = [
        pl.BlockSpec((tm, d), lambda i: (i, 0)),
        pl.BlockSpec((1, d), lambda i: (0, 0)),
        pl.BlockSpec((None, 2, d), mod_idx),
        pl.BlockSpec((d, n), lambda i: (0, 0)),
    ]
    args = [h, g, mod, w]
    rope = tables is not None
    if rope:
        for tb in tables:
            in_specs.append(pl.BlockSpec((tm, LANES), lambda i: (i % tiles_per_seq, 0)))
            args.append(tb)
    return pl.pallas_call(
        functools.partial(_qkv_kernel, rope=rope, n_rot=(q_cols + k_cols) // LANES, q_cols=q_cols,
                          scale=LOG2E * ATTN_HEAD_DIM ** -0.5),
        out_shape=jax.ShapeDtypeStruct((t, n), BF16),
        grid=(t // tm,),
        in_specs=in_specs,
        out_specs=pl.BlockSpec((tm, n), lambda i: (i, 0)),
        compiler_params=_cparams("parallel"),
        name="qkv_proj",
    )(*args)


def _outproj_kernel(a_ref, w_ref, h_ref, mod_ref, o_ref):
    y = jnp.dot(a_ref[...], w_ref[...], preferred_element_type=F32)
    o_ref[...] = h_ref[...] + mod_ref[...] * y


def _outproj(a, w, h, gate, *, seq, tm):
    t, k = a.shape
    d = w.shape[1]
    tiles_per_seq = seq // tm
    bm = gate.shape[0]
    mod_idx = (lambda i: (i // tiles_per_seq, 0, 0)) if bm > 1 else (lambda i: (0, 0, 0))
    return pl.pallas_call(
        _outproj_kernel,
        out_shape=jax.ShapeDtypeStruct((t, d), F32),
        grid=(t // tm,),
        in_specs=[
            pl.BlockSpec((tm, k), lambda i: (i, 0)),
            pl.BlockSpec((k, d), lambda i: (0, 0)),
            pl.BlockSpec((tm, d), lambda i: (i, 0)),
            pl.BlockSpec((None, 1, d), mod_idx),
        ],
        out_specs=pl.BlockSpec((tm, d), lambda i: (i, 0)),
        compiler_params=_cparams("parallel"),
        name="outproj",
    )(a, w, h, gate)


def _ffn_kernel(x_ref, g_ref, mod_ref, win_ref, wo_ref, *refs, final, hidden, chunk):
    if final:
        fg_ref, o_ref = refs
    else:
        (o_ref,) = refs
    x = x_ref[...]
    a = _norm_mod(x, g_ref[...], mod_ref[0:1, :], mod_ref[1:2, :]).astype(BF16)
    dot = functools.partial(jnp.dot, preferred_element_type=F32)
    acc = None
    for c0 in range(0, hidden, chunk):
        gte = dot(a, win_ref[:, c0:c0 + chunk])
        up = dot(a, win_ref[:, hidden + c0:hidden + c0 + chunk])
        part = dot((_silu(gte) * up).astype(BF16), wo_ref[c0:c0 + chunk, :])
        acc = part if acc is None else acc + part
    y = x + mod_ref[2:3, :] * acc
    if final:
        y = y * lax.rsqrt(jnp.mean(y * y, axis=-1, keepdims=True) + RMS_EPS) * fg_ref[...]
    o_ref[...] = y


def _ffn(h, g, mod, w_in, w_out, *, seq, tm, chunk, final_g=None):
    t, d = h.shape
    hidden = w_out.shape[0]
    tiles_per_seq = seq // tm
    bm = mod.shape[0]
    mod_idx = (lambda i: (i // tiles_per_seq, 0, 0)) if bm > 1 else (lambda i: (0, 0, 0))
    resident = pl.Buffered(1)
    in_specs = [
        pl.BlockSpec((tm, d), lambda i: (i, 0)),
        pl.BlockSpec((1, d), lambda i: (0, 0)),
        pl.BlockSpec((None, 3, d), mod_idx),
        pl.BlockSpec((d, 2 * hidden), lambda i: (0, 0), pipeline_mode=resident),
        pl.BlockSpec((hidden, d), lambda i: (0, 0), pipeline_mode=resident),
    ]
    args = [h, g, mod, w_in, w_out]
    if final_g is not None:
        in_specs.append(pl.BlockSpec((1, d), lambda i: (0, 0)))
        args.append(final_g)
    return pl.pallas_call(
        functools.partial(_ffn_kernel, final=final_g is not None, hidden=hidden, chunk=chunk),
        out_shape=jax.ShapeDtypeStruct((t, d), F32),
        grid=(t // tm,),
        in_specs=in_specs,
        out_specs=pl.BlockSpec((tm, d), lambda i: (i, 0)),
        compiler_params=_cparams("parallel"),
        name="ffn",
    )(*args)


HALO = 16


def _ssd_inproj_kernel(x_ref, xp_ref, xn_ref, g_ref, mod_ref, w_ref, wdt_ref, cw_ref, cb_ref,
                       z_ref, xc_ref, dt_ref, a_scr, *, tiles_per_seq, tm, d_inner, chunk):
    pos = pl.program_id(0) % tiles_per_seq
    g, shift, scale = g_ref[...], mod_ref[0:1, :], mod_ref[1:2, :]
    a_prev = jnp.where(pos == 0, 0.0, _norm_mod(xp_ref[...], g, shift, scale))
    a_next = jnp.where(pos == tiles_per_seq - 1, 0.0, _norm_mod(xn_ref[...], g, shift, scale))
    a_scr[0:HALO, :] = a_prev.astype(BF16)
    a_scr[HALO:HALO + tm, :] = _norm_mod(x_ref[...], g, shift, scale).astype(BF16)
    a_scr[HALO + tm:, :] = a_next.astype(BF16)
    dot = functools.partial(jnp.dot, preferred_element_type=F32)
    a_main = a_scr[HALO:HALO + tm, :]
    dt_ref[...] = dot(a_main, wdt_ref[...])

    rows = tm + 2 * HALO
    half = SSD_CONV_W // 2
    spc = chunk // LANES
    t0, nt = HALO // SUBLANES, tm // SUBLANES
    sub = lax.broadcasted_iota(jnp.int32, (1, SUBLANES, LANES), 1)
    for c0 in range(0, d_inner, chunk):
        z_ref[:, c0:c0 + chunk] = dot(a_main, w_ref[:, c0:c0 + chunk])
    for c in range(cw_ref.shape[0] // spc):
        pre = dot(a_scr[...], w_ref[:, d_inner + c * chunk:d_inner + (c + 1) * chunk])
        for s in range(spc):
            e = pre[:, s * LANES:(s + 1) * LANES].reshape(rows // SUBLANES, SUBLANES, LANES)
            cw = cw_ref[c * spc + s]
            acc = cb_ref[c * spc + s] + cw[half:half + 1, :] * e[t0:t0 + nt]
            rot = {1: pltpu.roll(e, 1, 1), 2: pltpu.roll(e, 2, 1)}
            rot[6] = pltpu.roll(rot[2], 4, 1)
            rot[7] = pltpu.roll(rot[6], 1, 1)
            for k in range(SSD_CONV_W):
                sh = half - k
                if sh == 0:
                    continue
                r = rot[sh % SUBLANES]
                if sh > 0:
                    tap = jnp.where(sub < sh, r[t0 - 1:t0 - 1 + nt], r[t0:t0 + nt])
                else:
                    tap = jnp.where(sub < SUBLANES + sh, r[t0:t0 + nt], r[t0 + 1:t0 + 1 + nt])
                acc = acc + cw[k:k + 1, :] * tap
            xc_ref[c * spc + s] = _silu(acc).reshape(tm, LANES)


def _ssd_inproj(h, g, mod, w_main, w_dt, conv_w, conv_b, *, seq, tm, d_inner, chunk):
    t, d = h.shape
    slabs = conv_w.shape[0]
    tiles_per_seq = seq // tm
    bm = mod.shape[0]
    mod_idx = (lambda i: (i // tiles_per_seq, 0, 0)) if bm > 1 else (lambda i: (0, 0, 0))
    rb = tm // HALO
    last_rb = t // HALO - 1
    resident = pl.Buffered(1)
    return pl.pallas_call(
        functools.partial(_ssd_inproj_kernel, tiles_per_seq=tiles_per_seq, tm=tm, d_inner=d_inner, chunk=chunk),
        out_shape=[jax.ShapeDtypeStruct((t, d_inner), F32),
                   jax.ShapeDtypeStruct((slabs, t, LANES), F32),
                   jax.ShapeDtypeStruct((t, w_dt.shape[1]), F32)],
        grid=(t // tm,),
        in_specs=[
            pl.BlockSpec((tm, d), lambda i: (i, 0)),
            pl.BlockSpec((HALO, d), lambda i: (jnp.maximum(i * rb - 1, 0), 0)),
            pl.BlockSpec((HALO, d), lambda i: (jnp.minimum((i + 1) * rb, last_rb), 0)),
            pl.BlockSpec((1, d), lambda i: (0, 0)),
            pl.BlockSpec((None, 2, d), mod_idx),
            pl.BlockSpec(w_main.shape, lambda i: (0, 0), pipeline_mode=resident),
            pl.BlockSpec(w_dt.shape, lambda i: (0, 0), pipeline_mode=resident),
            pl.BlockSpec(conv_w.shape, lambda i: (0, 0, 0)),
            pl.BlockSpec(conv_b.shape, lambda i: (0, 0, 0)),
        ],
        out_specs=[pl.BlockSpec((tm, d_inner), lambda i: (i, 0)),
                   pl.BlockSpec((slabs, tm, LANES), lambda i: (0, i, 0)),
                   pl.BlockSpec((tm, w_dt.shape[1]), lambda i: (i, 0))],
        scratch_shapes=[pltpu.VMEM((tm + 2 * HALO, d), BF16)],
        compiler_params=_cparams("parallel"),
        name="ssd_inproj",
    )(h, h, h, g, mod, w_main, w_dt, conv_w, conv_b)


def _ssd_kernel(x_ref, b_ref, c_ref, dt_ref, bias_ref, alog_ref, s0_ref, *refs,
                reverse, finish, heads, col_off):
    if finish:
        yf_ref, z_ref, d_ref, nw_ref, o_ref, sfin_ref, state, y_scr = refs
    else:
        o_ref, sfin_ref, state = refs
        y_scr = o_ref
    L = SSD_CHUNK
    hd = SSD_HEADDIM
    hpg = heads // SSD_GROUPS
    gw = hpg * hd
    pairs = gw // LANES
    j = pl.program_id(1)

    @pl.when(j == 0)
    def _():
        state[...] = s0_ref[...]

    row = lax.broadcasted_iota(jnp.int32, (L, L), 0)
    col = lax.broadcasted_iota(jnp.int32, (L, L), 1)
    mask = (col >= row) if reverse else (col <= row)
    tri = jnp.where(mask, 1.0, 0.0).astype(BF16)
    first = col < hd
    first_row = lax.broadcasted_iota(jnp.int32, (1, LANES), 1) < hd

    v = dt_ref[...] + bias_ref[...]
    dt = jnp.maximum(v, 0.0) + jnp.log1p(jnp.exp(-jnp.abs(v)))
    dta = dt * (-jnp.exp(alog_ref[...]))
    p1, p2, p3 = _split3(dta)
    dot = functools.partial(jnp.dot, preferred_element_type=F32)
    dotm = lambda a, b: lax.dot_general(a, b, (((1,), (0,)), ((), ())), preferred_element_type=F32)
    a_cum = dot(tri, p1) + (dot(tri, p2) + dot(tri, p3))
    b_log = a_cum - jnp.log(dt)
    a_tot = a_cum[0:1, :] if reverse else a_cum[L - 1:L, :]
    a_cum2 = a_cum * LOG2E
    b_t = (b_log * LOG2E).T
    dtw_t = jnp.exp(a_tot - b_log).T
    e_tot = jnp.exp(a_tot)

    for g in range(SSD_GROUPS):
        bg = b_ref[g]
        cg = c_ref[g].astype(BF16)
        cb = lax.dot_general(cg, bg.astype(BF16), (((1,), (1,)), ((), ())), preferred_element_type=F32)
        bg_t = bg.T
        y_in = dot(cg, state[:, g * gw:(g + 1) * gw].astype(BF16))
        for k in range(pairs):
            ca = col_off + g * hpg + 2 * k
            lo = g * gw + k * LANES
            xp = x_ref[g * pairs + k]
            x_ab = jnp.concatenate([jnp.where(first, xp, 0.0), jnp.where(first, 0.0, xp)], axis=0).astype(BF16)
            a_a = jnp.broadcast_to(a_cum2[:, ca:ca + 1], (L, L))
            a_b = jnp.broadcast_to(a_cum2[:, ca + 1:ca + 2], (L, L))
            m_a = cb * jnp.exp2(jnp.where(mask, a_a - b_t[ca:ca + 1, :], -jnp.inf))
            m_b = cb * jnp.exp2(jnp.where(mask, a_b - b_t[ca + 1:ca + 2, :], -jnp.inf))
            e_cum = jnp.exp2(jnp.where(first, a_a, a_b))
            y_pair = dotm(jnp.concatenate([m_a, m_b], axis=1), x_ab) + y_in[:, k * LANES:(k + 1) * LANES] * e_cum
            if finish:
                y_pair = y_pair + xp * d_ref[:, lo:lo + LANES]
            y_scr[:, lo:lo + LANES] = y_pair
            bt_a = bg_t * dtw_t[ca:ca + 1, :]
            bt_b = bg_t * dtw_t[ca + 1:ca + 2, :]
            e_end = jnp.where(first_row, e_tot[:, ca:ca + 1], e_tot[:, ca + 1:ca + 2])
            state[:, lo:lo + LANES] = (state[:, lo:lo + LANES] * e_end
                                       + dotm(jnp.concatenate([bt_a, bt_b], axis=1), x_ab))

    if finish:
        y = y_scr[...] + yf_ref[...]
        gated = y * _silu(z_ref[...])
        gn = gated * lax.rsqrt(jnp.mean(gated * gated, axis=-1, keepdims=True) + RMS_EPS) * nw_ref[...]
        o_ref[...] = gn.astype(o_ref.dtype)

    @pl.when(j == pl.num_programs(1) - 1)
    def _():
        sfin_ref[...] = state[...]


def _ssd_scan(xc, dt_raw, dt_bias, a_log, state0, *, batch, seq, reverse, d_inner, finish_args=None):
    t = xc.shape[1]
    L = SSD_CHUNK
    nc = seq // L
    heads = d_inner // SSD_HEADDIM
    x_slabs = d_inner // LANES
    gs = SSD_GROUPS * SSD_STATE // LANES
    assert SSD_STATE == LANES and x_slabs % gs == 0
    finish = finish_args is not None

    def rows(b, j):
        return b * nc + ((nc - 1 - j) if reverse else j)

    in_specs = [
        pl.BlockSpec((x_slabs, L, LANES), lambda b, j: (0, rows(b, j), 0)),
        pl.BlockSpec((gs, L, LANES), lambda b, j: (x_slabs // gs, rows(b, j), 0)),
        pl.BlockSpec((gs, L, LANES), lambda b, j: (x_slabs // gs + 1, rows(b, j), 0)),
        pl.BlockSpec((L, LANES), lambda b, j: (rows(b, j), 0)),
        pl.BlockSpec((1, LANES), lambda b, j: (0, 0)),
        pl.BlockSpec((1, LANES), lambda b, j: (0, 0)),
        pl.BlockSpec((None, SSD_STATE, d_inner), lambda b, j: (b, 0, 0)),
    ]
    args = [xc, xc, xc, dt_raw, dt_bias, a_log, state0]
    if finish:
        y_f, z, d_exp, norm_w = finish_args
        in_specs += [
            pl.BlockSpec((L, d_inner), lambda b, j: (rows(b, j), 0)),
            pl.BlockSpec((L, d_inner), lambda b, j: (rows(b, j), 0)),
            pl.BlockSpec((1, d_inner), lambda b, j: (0, 0)),
            pl.BlockSpec((1, d_inner), lambda b, j: (0, 0)),
        ]
        args += [y_f, z, d_exp, norm_w]
    out_dtype = BF16 if finish else F32
    return pl.pallas_call(
        functools.partial(_ssd_kernel, reverse=reverse, finish=finish, heads=heads,
                          col_off=heads if reverse else 0),
        out_shape=[jax.ShapeDtypeStruct((t, d_inner), out_dtype),
                   jax.ShapeDtypeStruct((batch, SSD_STATE, d_inner), F32)],
        grid=(batch, nc),
        in_specs=in_specs,
        out_specs=[pl.BlockSpec((L, d_inner), lambda b, j: (rows(b, j), 0)),
                   pl.BlockSpec((None, SSD_STATE, d_inner), lambda b, j: (b, 0, 0))],
        scratch_shapes=[pltpu.VMEM((SSD_STATE, d_inner), F32)]
        + ([pltpu.VMEM((L, d_inner), F32)] if finish else []),
        compiler_params=_cparams("parallel", "arbitrary"),
        name="ssd_scan_bwd" if reverse else "ssd_scan_fwd",
    )(*args)


def _attn_kernel(sink_ref, q_ref, kvc_ref, wo_ref, h_ref, gate_ref, *refs, band, q_heads):
    if band:
        kvp_ref, kv0_ref, kvn_ref, o_ref = refs
    else:
        (o_ref,) = refs
    hd = ATTN_HEAD_DIM
    kd = ATTN_KV_HEADS * hd
    group = q_heads // ATTN_KV_HEADS
    blk = q_ref.shape[0]
    dot = functools.partial(jnp.dot, preferred_element_type=F32)
    dot_t = lambda a, b: lax.dot_general(a, b, (((1,), (1,)), ((), ())), preferred_element_type=F32)
    gq = group * blk
    key_idx = lax.broadcasted_iota(jnp.int32, (blk, gq), 0)
    qry_idx = lax.broadcasted_iota(jnp.int32, (blk, gq), 1) % blk
    lane_head = lax.broadcasted_iota(jnp.int32, (1, gq), 1) // blk
    segs = [(kvc_ref[c0:c0 + blk, :kd], kvc_ref[c0:c0 + blk, kd:], None) for c0 in range(0, kvc_ref.shape[0], blk)]
    if band:
        j = pl.program_id(1)
        ok_prev = key_idx >= qry_idx + jnp.where(j > 0, 0, 2 * blk)
        ok_next = key_idx + jnp.where(j < pl.num_programs(1) - 1, 0, 2 * blk) <= qry_idx
        segs += [(kvp_ref[:, :kd], kvp_ref[:, kd:], ok_prev), (kv0_ref[:, :kd], kv0_ref[:, kd:], None),
                 (kvn_ref[:, :kd], kvn_ref[:, kd:], ok_next)]
    v_t = [v.astype(F32).T.astype(BF16) for _, v, _ in segs]
    o_rows = []
    for kv in range(ATTN_KV_HEADS):
        ksl = slice(kv * hd, (kv + 1) * hd)
        h0 = kv * group
        qs = jnp.concatenate([q_ref[:, (h0 + i) * hd:(h0 + i + 1) * hd] for i in range(group)], axis=0)
        sink = jnp.full((1, gq), sink_ref[h0] * LOG2E, F32)
        for i in range(1, group):
            sink = jnp.where(lane_head == i, sink_ref[h0 + i] * LOG2E, sink)
        s_tiles = []
        for k, _, ok in segs:
            s = dot_t(k[:, ksl], qs)
            s_tiles.append(s if ok is None else jnp.where(ok, s, -jnp.inf))
        m = jnp.maximum(jnp.max(functools.reduce(jnp.maximum, s_tiles), axis=0, keepdims=True), sink)
        p_tiles = [jnp.exp2(s - m) for s in s_tiles]
        den = jnp.exp2(sink - m) + jnp.sum(functools.reduce(jnp.add, p_tiles), axis=0, keepdims=True)
        o_t = functools.reduce(jnp.add, [dot(vt[ksl, :], p.astype(BF16)) for vt, p in zip(v_t, p_tiles)])
        o_t = o_t * (1.0 / den)
        o_rows += [o_t[:, i * blk:(i + 1) * blk] for i in range(group)]
    o = jnp.concatenate(o_rows, axis=0).T.astype(BF16)
    o_ref[...] = h_ref[...] + gate_ref[...] * dot(o, wo_ref[...])


def _attention(sink, qkv, qkv_ctx, w_o, h, gate, *, batch, seq, n_ctx, q_heads, band):
    blk = ATTN_BLOCK
    assert n_ctx % blk == 0 and seq % blk == 0
    nb = seq // blk
    d = h.shape[1]
    qd = q_heads * ATTN_HEAD_DIM
    kd = ATTN_KV_HEADS * ATTN_HEAD_DIM
    assert qd % (2 * kd) == 0
    kvcol = qd // (2 * kd)
    gate_idx = (lambda b, j: (b, 0, 0)) if gate.shape[0] > 1 else (lambda b, j: (0, 0, 0))
    in_specs = [
        pl.BlockSpec(memory_space=pltpu.SMEM),
        pl.BlockSpec((blk, qd), lambda b, j: (b * nb + j, 0)),
        pl.BlockSpec((n_ctx, 2 * kd), lambda b, j: (b, kvcol)),
        pl.BlockSpec(w_o.shape, lambda b, j: (0, 0), pipeline_mode=pl.Buffered(1)),
        pl.BlockSpec((blk, d), lambda b, j: (b * nb + j, 0)),
        pl.BlockSpec((None, 1, d), gate_idx),
    ]
    args = [sink, qkv, qkv_ctx, w_o, h, gate]
    if band:
        prev = lambda b, j: b * nb + jnp.maximum(j - 1, 0)
        cur = lambda b, j: b * nb + j
        nxt = lambda b, j: b * nb + jnp.minimum(j + 1, nb - 1)
        for f in (prev, cur, nxt):
            in_specs.append(pl.BlockSpec((blk, 2 * kd), functools.partial(lambda b, j, f: (f(b, j), kvcol), f=f)))
            args.append(qkv)
    return pl.pallas_call(
        functools.partial(_attn_kernel, band=band, q_heads=q_heads),
        out_shape=jax.ShapeDtypeStruct((batch * seq, d), F32),
        grid=(batch, nb),
        in_specs=in_specs,
        out_specs=pl.BlockSpec((blk, d), lambda b, j: (b * nb + j, 0)),
        compiler_params=_cparams("parallel", "parallel"),
        name="window_attn" if band else "ctx_attn",
    )(*args)


def _gmlp_kernel(x_ref, g_ref, mod_ref, win_ref, lg_ref, lb_ref, ws_ref, bs_ref, wo_ref, o_ref, gm_scr, *, chunk):
    x = x_ref[...]
    a = _norm_mod(x, g_ref[...], mod_ref[0:1, :], mod_ref[1:2, :]).astype(BF16)
    width = wo_ref.shape[0]
    dot = functools.partial(jnp.dot, preferred_element_type=F32)

    def gelu_proj(c0):
        y = dot(a, win_ref[:, c0:c0 + chunk])
        return 0.5 * y * (1.0 + lax.erf(y * (1.0 / math.sqrt(2.0))))

    v = jnp.concatenate([gelu_proj(width + c0) for c0 in range(0, width, chunk)], axis=1)
    mu = jnp.mean(v, axis=-1, keepdims=True)
    vc = v - mu
    var = jnp.mean(vc * vc, axis=-1, keepdims=True)
    vn = (vc * lax.rsqrt(var + LN_EPS) * lg_ref[...] + lb_ref[...]).astype(BF16)
    gd = width // GMLP_GROUPS
    ch = GMLP_CHUNK
    for c0 in range(0, width, chunk):
        u = gelu_proj(c0)
        for g in range(c0 // gd, (c0 + chunk) // gd):
            for r in range(0, x.shape[0], ch):
                sv = dot(ws_ref[g], vn[r:r + ch, g * gd:(g + 1) * gd]) + bs_ref[:, g:g + 1]
                gm_scr[r:r + ch, g * gd:(g + 1) * gd] = (u[r:r + ch, g * gd - c0:(g + 1) * gd - c0] * sv).astype(BF16)
    o_ref[...] = x + mod_ref[2:3, :] * dot(gm_scr[...], wo_ref[...])


def _gmlp(h, g, mod, w_in, ln_g, ln_b, w_s, b_s_t, w_out, *, seq, tm, chunk):
    t, d = h.shape
    width = w_out.shape[0]
    assert tm % GMLP_CHUNK == 0 and chunk % (width // GMLP_GROUPS) == 0
    tiles_per_seq = seq // tm
    bm = mod.shape[0]
    mod_idx = (lambda i: (i // tiles_per_seq, 0, 0)) if bm > 1 else (lambda i: (0, 0, 0))
    resident = pl.Buffered(1)
    return pl.pallas_call(
        functools.partial(_gmlp_kernel, chunk=chunk),
        out_shape=jax.ShapeDtypeStruct((t, d), F32),
        grid=(t // tm,),
        in_specs=[
            pl.BlockSpec((tm, d), lambda i: (i, 0)),
            pl.BlockSpec((1, d), lambda i: (0, 0)),
            pl.BlockSpec((None, 3, d), mod_idx),
            pl.BlockSpec(w_in.shape, lambda i: (0, 0), pipeline_mode=resident),
            pl.BlockSpec((1, width), lambda i: (0, 0)),
            pl.BlockSpec((1, width), lambda i: (0, 0)),
            pl.BlockSpec(w_s.shape, lambda i: (0, 0, 0)),
            pl.BlockSpec(b_s_t.shape, lambda i: (0, 0)),
            pl.BlockSpec(w_out.shape, lambda i: (0, 0), pipeline_mode=resident),
        ],
        out_specs=pl.BlockSpec((tm, d), lambda i: (i, 0)),
        scratch_shapes=[pltpu.VMEM((tm, width), BF16)],
        compiler_params=_cparams("parallel"),
        name="gmlp",
    )(h, g, mod, w_in, ln_g, ln_b, w_s, b_s_t, w_out)


def _rope_tables(n):
    freqs = ATTN_HEAD_DIM // 4
    rows = n // GRID_W
    row = jnp.repeat(jnp.arange(rows, dtype=jnp.int32), GRID_W, total_repeat_length=n)
    col = jnp.tile(jnp.arange(GRID_W, dtype=jnp.int32), rows)
    inv = ROPE_BASE ** (-jnp.arange(freqs, dtype=F32) / freqs)
    ang = jnp.stack([row.astype(F32)[:, None] * inv, col.astype(F32)[:, None] * inv], axis=1)
    ang = jnp.repeat(ang[:, :, None, :], 2, axis=2).reshape(n, ATTN_HEAD_DIM)
    cos, sin = jnp.cos(ang), jnp.sin(ang)
    reps = LANES // ATTN_HEAD_DIM
    cos, sin = jnp.tile(cos, (1, reps)), jnp.tile(sin, (1, reps))
    first_half = jnp.asarray((np.arange(LANES) // freqs) % 2 == 0)[None, :]
    return cos, jnp.where(first_half, -sin, 0.0), jnp.where(first_half, 0.0, sin)


def _ssd_mixer(streams, norm_g, w_in, conv_w, conv_b, a_log, dt_bias, d_skip, norm_w, w_out, need_ctx_out):
    d_inner = norm_w.shape[0]
    conv_dim = conv_w.shape[1]
    heads = d_inner // SSD_HEADDIM
    w_main = w_in[:, :d_inner + conv_dim].astype(BF16)
    w_dt = jnp.pad(w_in[:, d_inner + conv_dim:], ((0, 0), (0, LANES - 2 * heads))).astype(BF16)
    bias = jnp.pad(dt_bias.reshape(1, 2 * heads), ((0, 0), (0, LANES - 2 * heads)))
    alog = jnp.pad(a_log.reshape(1, 2 * heads), ((0, 0), (0, LANES - 2 * heads)))
    d_exp = jnp.repeat(d_skip, SSD_HEADDIM).reshape(1, d_inner)
    nw = norm_w.reshape(1, d_inner)
    w_out = w_out.astype(BF16)
    slabs = conv_dim // LANES
    conv_w = jnp.pad(conv_w, ((0, SUBLANES - SSD_CONV_W), (0, 0))).reshape(SUBLANES, slabs, LANES).transpose(1, 0, 2)
    conv_b = conv_b.reshape(slabs, 1, LANES)

    outs = []
    s_f = s_b = None
    for name, s in streams:
        if s_f is None:
            s_f = s_b = jnp.zeros((s["batch"], SSD_STATE, d_inner), F32)
        z, xc, dt_raw = _ssd_inproj(s["h"], norm_g, s["mod"][:, 0:2], w_main, w_dt, conv_w, conv_b,
                                    seq=s["seq"], tm=s["tm"], d_inner=d_inner, chunk=1024)
        y_f, s_f = _ssd_scan(xc, dt_raw, bias, alog, s_f, batch=s["batch"], seq=s["seq"], reverse=False,
                             d_inner=d_inner)
        gn, s_b = _ssd_scan(xc, dt_raw, bias, alog, s_b, batch=s["batch"], seq=s["seq"], reverse=True,
                            d_inner=d_inner, finish_args=(y_f, z, d_exp, nw))
        if name == "ctx" and not need_ctx_out:
            outs.append(None)
        else:
            tm_out = 2 * s["tm"] if s["seq"] % (2 * s["tm"]) == 0 else s["tm"]
            outs.append(_outproj(gn, w_out, s["h"], s["mod"][:, 2:3], seq=s["seq"], tm=tm_out))
    return outs


def _attn_mixer(streams, norm_g, w_qkv, sink, w_o, tables, need_ctx_out):
    (_, sc), (_, sl) = streams
    q_heads = sink.shape[0]
    qd = q_heads * ATTN_HEAD_DIM
    kd = ATTN_KV_HEADS * ATTN_HEAD_DIM
    w_qkv = w_qkv.astype(BF16)
    w_o = w_o.astype(BF16)
    qkv_c = _qkv_proj(sc["h"], norm_g, sc["mod"][:, 0:2], w_qkv, None, seq=sc["seq"], tm=sc["tm"],
                      q_cols=qd, k_cols=kd)
    qkv_l = _qkv_proj(sl["h"], norm_g, sl["mod"][:, 0:2], w_qkv, tables, seq=sl["seq"], tm=sl["tm"],
                      q_cols=qd, k_cols=kd)
    h_l = _attention(sink, qkv_l, qkv_c, w_o, sl["h"], sl["mod"][:, 2:3], batch=sl["batch"], seq=sl["seq"],
                     n_ctx=sc["seq"], q_heads=q_heads, band=True)
    h_c = None
    if need_ctx_out:
        h_c = _attention(sink, qkv_c, qkv_c, w_o, sc["h"], sc["mod"][:, 2:3], batch=sc["batch"], seq=sc["seq"],
                         n_ctx=sc["seq"], q_heads=q_heads, band=False)
    return [h_c, h_l]


def _gmlp_mixer(streams, norm_g, w_in, ln_g, ln_b, w_s, b_s, w_out, need_ctx_out):
    width = ln_g.shape[0]
    w_in = w_in.astype(BF16)
    w_out = w_out.astype(BF16)
    w_s = w_s.astype(BF16)
    outs = []
    for name, s in streams:
        if name == "ctx" and not need_ctx_out:
            outs.append(None)
            continue
        outs.append(_gmlp(s["h"], norm_g, s["mod"][:, 0:3], w_in, ln_g.reshape(1, width), ln_b.reshape(1, width),
                          w_s, b_s.T, w_out, seq=s["seq"], tm=s["tm"], chunk=1024))
    return outs


def kernel(x, c, ctx, c_ctx, w_mod, b_mod, norm_g, final_g, ssd_w_in, ssd_conv_w, ssd_conv_b, ssd_a_log, ssd_dt_bias, ssd_d, ssd_norm_w, ssd_w_out, attn_w_qkv, attn_sink, attn_w_o, gmlp_w_in, gmlp_ln_g, gmlp_ln_b, gmlp_w_s, gmlp_b_s, gmlp_w_out, ffn_w_in, ffn_w_out):
    batch, seq, d = x.shape
    n_ctx = ctx.shape[1]
    depth = w_mod.shape[0]
    hidden = ffn_w_out.shape[1]

    pad_rows = -(batch + 1) % 16
    cc = jnp.concatenate([c, c_ctx[None, :], jnp.zeros((pad_rows, d), F32)], axis=0)
    mods = _modulation(cc, w_mod, b_mod)
    tables = _rope_tables(seq)

    h_lat = x.reshape(batch * seq, d)
    h_ctx = ctx.reshape(batch * n_ctx, d)
    tm_lat = 512 if seq % 512 == 0 else 256
    tm_ctx = 256
    ffn_chunk = hidden

    for i in range(depth):
        last = i == depth - 1
        kind, j = i % 3, i // 3
        mod_lat = mods[i, :batch].reshape(batch, N_MOD, d)
        mod_ctx = mods[i, batch:batch + 1].reshape(1, N_MOD, d)
        g1 = norm_g[i, 0].reshape(1, d)
        g2 = norm_g[i, 1].reshape(1, d)
        streams = [
            ("ctx", dict(h=h_ctx, mod=mod_ctx, batch=batch, seq=n_ctx, tm=tm_ctx)),
            ("lat", dict(h=h_lat, mod=mod_lat, batch=batch, seq=seq, tm=tm_lat)),
        ]
        need_ctx_out = not last
        if kind == 0:
            h_ctx_new, h_lat = _ssd_mixer(streams, g1, ssd_w_in[j], ssd_conv_w[j], ssd_conv_b[j], ssd_a_log[j],
                                          ssd_dt_bias[j], ssd_d[j], ssd_norm_w[j], ssd_w_out[j], need_ctx_out)
        elif kind == 1:
            h_ctx_new, h_lat = _attn_mixer(streams, g1, attn_w_qkv[j], attn_sink[j], attn_w_o[j], tables,
                                           need_ctx_out)
        else:
            h_ctx_new, h_lat = _gmlp_mixer(streams, g1, gmlp_w_in[j], gmlp_ln_g[j], gmlp_ln_b[j], gmlp_w_s[j],
                                           gmlp_b_s[j], gmlp_w_out[j], need_ctx_out)

        w_in = ffn_w_in[i].astype(BF16)
        w_out = ffn_w_out[i].astype(BF16)
        h_lat = _ffn(h_lat, g2, mod_lat[:, 3:6], w_in, w_out, seq=seq, tm=tm_lat, chunk=ffn_chunk,
                     final_g=final_g.reshape(1, d) if last else None)
        if need_ctx_out:
            h_ctx = _ffn(h_ctx_new, g2, mod_ctx[:, 3:6], w_in, w_out, seq=n_ctx, tm=tm_ctx, chunk=ffn_chunk)
    return h_lat.reshape(batch, seq, d)
```

```python
import functools
import math

import jax
import jax.numpy as jnp
import numpy as np
from jax import lax
from jax.experimental import pallas as pl
from jax.experimental.pallas import tpu as pltpu

F32 = jnp.float32
BF16 = jnp.bfloat16

N_MOD = 6
GRID_W = 64
SSD_HEADDIM = 64
SSD_GROUPS = 4
SSD_STATE = 128
SSD_CONV_W = 5
SSD_CHUNK = 128
ATTN_HEAD_DIM = 64
ATTN_KV_HEADS = 4
ATTN_WINDOW = 128
ATTN_BLOCK = 128
ROPE_BASE = 10000.0
GMLP_GROUPS = 8
GMLP_CHUNK = 128
RMS_EPS = 1e-6
LOG2E = math.log2(math.e)
LN_EPS = 1e-5

LANES = 128
SUBLANES = 8
VMEM_LIMIT = 56 * 1024 * 1024


def _cparams(*sem):
    return pltpu.CompilerParams(dimension_semantics=sem, vmem_limit_bytes=VMEM_LIMIT)


def _silu(x):
    return x * (1.0 / (1.0 + jnp.exp2(x * -LOG2E)))


def _norm_mod(x, g, shift, scale):
    y = x * lax.rsqrt(jnp.mean(x * x, axis=-1, keepdims=True) + RMS_EPS)
    return (y * g) * (1.0 + scale) + shift


def _split3(x):
    hi = x.astype(BF16)
    r1 = x - hi.astype(F32)
    mid = r1.astype(BF16)
    lo = (r1 - mid.astype(F32)).astype(BF16)
    return hi, mid, lo


def _mod_kernel(c_ref, w_ref, b_ref, o_ref):
    x = _silu(c_ref[...])
    w = w_ref[...]
    xh = x.astype(BF16)
    xl = (x - xh.astype(F32)).astype(BF16)
    wh = w.astype(BF16)
    wl = (w - wh.astype(F32)).astype(BF16)
    dot = functools.partial(jnp.dot, preferred_element_type=F32)
    o_ref[...] = dot(xh, wh) + (dot(xh, wl) + dot(xl, wh)) + b_ref[...]


def _modulation(cc, w_mod, b_mod):
    depth, d, n = w_mod.shape
    rows = cc.shape[0]
    tn = 1536
    return pl.pallas_call(
        _mod_kernel,
        out_shape=jax.ShapeDtypeStruct((depth, rows, n), F32),
        grid=(depth, n // tn),
        in_specs=[
            pl.BlockSpec((rows, d), lambda l, j: (0, 0)),
            pl.BlockSpec((None, d, tn), lambda l, j: (l, 0, j)),
            pl.BlockSpec((None, 1, tn), lambda l, j: (l, 0, j)),
        ],
        out_specs=pl.BlockSpec((None, rows, tn), lambda l, j: (l, 0, j)),
        compiler_params=_cparams("parallel", "parallel"),
        name="modulation",
    )(cc, w_mod, b_mod.reshape(depth, 1, n))


def _qkv_kernel(x_ref, g_ref, mod_ref, w_ref, *refs, rope, n_rot, q_cols, scale):
    if rope:
        cos_ref, sa_ref, sb_ref, o_ref = refs
    else:
        (o_ref,) = refs
    a = _norm_mod(x_ref[...], g_ref[...], mod_ref[0:1, :], mod_ref[1:2, :]).astype(BF16)
    y = jnp.dot(a, w_ref[...], preferred_element_type=F32)
    n = y.shape[1]
    if rope:
        cos, sa, sb = cos_ref[...], sa_ref[...], sb_ref[...]
    for s in range(n // LANES):
        ys = y[:, s * LANES:(s + 1) * LANES]
        if rope and s < n_rot:
            ys = ys * cos + pltpu.roll(ys, LANES - 16, 1) * sa + pltpu.roll(ys, 16, 1) * sb
        if s * LANES < q_cols:
            ys = ys * scale
        o_ref[:, s * LANES:(s + 1) * LANES] = ys.astype(o_ref.dtype)


def _qkv_proj(h, g, mod, w, tables, *, seq, tm, q_cols, k_cols):
    t, d = h.shape
    n = w.shape[1]
    tiles_per_seq = seq // tm
    bm = mod.shape[0]
    mod_idx = (lambda i: (i // tiles_per_seq, 0, 0)) if bm > 1 else (lambda i: (0, 0, 0))
    in_specs = [
        pl.BlockSpec((tm, d), lambda i: (i, 0)),
        pl.BlockSpec((1, d), lambda i: (0, 0)),
        pl.BlockSpec((None, 2, d), mod_idx),
        pl.BlockSpec((d, n), lambda i: (0, 0)),
    ]
    args = [h, g, mod, w]
    rope = tables is not None
    if rope:
        for tb in tables:
            in_specs.append(pl.BlockSpec((tm, LANES), lambda i: (i % tiles_per_seq, 0)))
            args.append(tb)
    return pl.pallas_call(
        functools.partial(_qkv_kernel, rope=rope, n_rot=(q_cols + k_cols) // LANES, q_cols=q_cols,
                          scale=ATTN_HEAD_DIM ** -0.5),
        out_shape=jax.ShapeDtypeStruct((t, n), BF16),
        grid=(t // tm,),
        in_specs=in_specs,
        out_specs=pl.BlockSpec((tm, n), lambda i: (i, 0)),
        compiler_params=_cparams("parallel"),
        name="qkv_proj",
    )(*args)


def _outproj_kernel(a_ref, w_ref, h_ref, mod_ref, o_ref):
    y = jnp.dot(a_ref[...], w_ref[...], preferred_element_type=F32)
    o_ref[...] = h_ref[...] + mod_ref[...] * y


def _outproj(a, w, h, gate, *, seq, tm):
    t, k = a.shape
    d = w.shape[1]
    tiles_per_seq = seq // tm
    bm = gate.shape[0]
    mod_idx = (lambda i: (i // tiles_per_seq, 0, 0)) if bm > 1 else (lambda i: (0, 0, 0))
    return pl.pallas_call(
        _outproj_kernel,
        out_shape=jax.ShapeDtypeStruct((t, d), F32),
        grid=(t // tm,),
        in_specs=[
            pl.BlockSpec((tm, k), lambda i: (i, 0)),
            pl.BlockSpec((k, d), lambda i: (0, 0)),
            pl.BlockSpec((tm, d), lambda i: (i, 0)),
            pl.BlockSpec((None, 1, d), mod_idx),
        ],
        out_specs=pl.BlockSpec((tm, d), lambda i: (i, 0)),
        compiler_params=_cparams("parallel"),
        name="outproj",
    )(a, w, h, gate)


def _ffn_kernel(x_ref, g_ref, mod_ref, win_ref, wo_ref, *refs, final, hidden, chunk):
    if final:
        fg_ref, o_ref = refs
    else:
        (o_ref,) = refs
    x = x_ref[...]
    a = _norm_mod(x, g_ref[...], mod_ref[0:1, :], mod_ref[1:2, :]).astype(BF16)
    dot = functools.partial(jnp.dot, preferred_element_type=F32)
    acc = None
    for c0 in range(0, hidden, chunk):
        gte = dot(a, win_ref[:, c0:c0 + chunk])
        up = dot(a, win_ref[:, hidden + c0:hidden + c0 + chunk])
        part = dot((_silu(gte) * up).astype(BF16), wo_ref[c0:c0 + chunk, :])
        acc = part if acc is None else acc + part
    y = x + mod_ref[2:3, :] * acc
    if final:
        y = y * lax.rsqrt(jnp.mean(y * y, axis=-1, keepdims=True) + RMS_EPS) * fg_ref[...]
    o_ref[...] = y


def _ffn(h, g, mod, w_in, w_out, *, seq, tm, chunk, final_g=None):
    t, d = h.shape
    hidden = w_out.shape[0]
    tiles_per_seq = seq // tm
    bm = mod.shape[0]
    mod_idx = (lambda i: (i // tiles_per_seq, 0, 0)) if bm > 1 else (lambda i: (0, 0, 0))
    resident = pl.Buffered(1)
    in_specs = [
        pl.BlockSpec((tm, d), lambda i: (i, 0)),
        pl.BlockSpec((1, d), lambda i: (0, 0)),
        pl.BlockSpec((None, 3, d), mod_idx),
        pl.BlockSpec((d, 2 * hidden), lambda i: (0, 0), pipeline_mode=resident),
        pl.BlockSpec((hidden, d), lambda i: (0, 0), pipeline_mode=resident),
    ]
    args = [h, g, mod, w_in, w_out]
    if final_g is not None:
        in_specs.append(pl.BlockSpec((1, d), lambda i: (0, 0)))
        args.append(final_g)
    return pl.pallas_call(
        functools.partial(_ffn_kernel, final=final_g is not None, hidden=hidden, chunk=chunk),
        out_shape=jax.ShapeDtypeStruct((t, d), F32),
        grid=(t // tm,),
        in_specs=in_specs,
        out_specs=pl.BlockSpec((tm, d), lambda i: (i, 0)),
        compiler_params=_cparams("parallel"),
        name="ffn",
    )(*args)


HALO = 16


def _ssd_inproj_kernel(x_ref, xp_ref, xn_ref, g_ref, mod_ref, w_ref, wdt_ref, cw_ref, cb_ref,
                       z_ref, xc_ref, dt_ref, a_scr, *, tiles_per_seq, tm, d_inner, chunk):
    pos = pl.program_id(0) % tiles_per_seq
    g, shift, scale = g_ref[...], mod_ref[0:1, :], mod_ref[1:2, :]
    a_prev = jnp.where(pos == 0, 0.0, _norm_mod(xp_ref[...], g, shift, scale))
    a_next = jnp.where(pos == tiles_per_seq - 1, 0.0, _norm_mod(xn_ref[...], g, shift, scale))
    a_scr[0:HALO, :] = a_prev.astype(BF16)
    a_scr[HALO:HALO + tm, :] = _norm_mod(x_ref[...], g, shift, scale).astype(BF16)
    a_scr[HALO + tm:, :] = a_next.astype(BF16)
    dot = functools.partial(jnp.dot, preferred_element_type=F32)
    a_main = a_scr[HALO:HALO + tm, :]
    dt_ref[...] = dot(a_main, wdt_ref[...])

    rows = tm + 2 * HALO
    half = SSD_CONV_W // 2
    spc = chunk // LANES
    t0, nt = HALO // SUBLANES, tm // SUBLANES
    sub = lax.broadcasted_iota(jnp.int32, (1, SUBLANES, LANES), 1)
    for c0 in range(0, d_inner, chunk):
        z_ref[:, c0:c0 + chunk] = dot(a_main, w_ref[:, c0:c0 + chunk])
    for c in range(cw_ref.shape[0] // spc):
        pre = dot(a_scr[...], w_ref[:, d_inner + c * chunk:d_inner + (c + 1) * chunk])
        for s in range(spc):
            e = pre[:, s * LANES:(s + 1) * LANES].reshape(rows // SUBLANES, SUBLANES, LANES)
            cw = cw_ref[c * spc + s]
            acc = cb_ref[c * spc + s] + cw[half:half + 1, :] * e[t0:t0 + nt]
            rot = {1: pltpu.roll(e, 1, 1), 2: pltpu.roll(e, 2, 1)}
            rot[6] = pltpu.roll(rot[2], 4, 1)
            rot[7] = pltpu.roll(rot[6], 1, 1)
            for k in range(SSD_CONV_W):
                sh = half - k
                if sh == 0:
                    continue
                r = rot[sh % SUBLANES]
                if sh > 0:
                    tap = jnp.where(sub < sh, r[t0 - 1:t0 - 1 + nt], r[t0:t0 + nt])
                else:
                    tap = jnp.where(sub < SUBLANES + sh, r[t0:t0 + nt], r[t0 + 1:t0 + 1 + nt])
                acc = acc + cw[k:k + 1, :] * tap
            xc_ref[c * spc + s] = _silu(acc).reshape(tm, LANES)


def _ssd_inproj(h, g, mod, w_main, w_dt, conv_w, conv_b, *, seq, tm, d_inner, chunk):
    t, d = h.shape
    slabs = conv_w.shape[0]
    tiles_per_seq = seq // tm
    bm = mod.shape[0]
    mod_idx = (lambda i: (i // tiles_per_seq, 0, 0)) if bm > 1 else (lambda i: (0, 0, 0))
    rb = tm // HALO
    last_rb = t // HALO - 1
    resident = pl.Buffered(1)
    return pl.pallas_call(
        functools.partial(_ssd_inproj_kernel, tiles_per_seq=tiles_per_seq, tm=tm, d_inner=d_inner, chunk=chunk),
        out_shape=[jax.ShapeDtypeStruct((t, d_inner), F32),
                   jax.ShapeDtypeStruct((slabs, t, LANES), F32),
                   jax.ShapeDtypeStruct((t, w_dt.shape[1]), F32)],
        grid=(t // tm,),
        in_specs=[
            pl.BlockSpec((tm, d), lambda i: (i, 0)),
            pl.BlockSpec((HALO, d), lambda i: (jnp.maximum(i * rb - 1, 0), 0)),
            pl.BlockSpec((HALO, d), lambda i: (jnp.minimum((i + 1) * rb, last_rb), 0)),
            pl.BlockSpec((1, d), lambda i: (0, 0)),
            pl.BlockSpec((None, 2, d), mod_idx),
            pl.BlockSpec(w_main.shape, lambda i: (0, 0), pipeline_mode=resident),
            pl.BlockSpec(w_dt.shape, lambda i: (0, 0), pipeline_mode=resident),
            pl.BlockSpec(conv_w.shape, lambda i: (0, 0, 0)),
            pl.BlockSpec(conv_b.shape, lambda i: (0, 0, 0)),
        ],
        out_specs=[pl.BlockSpec((tm, d_inner), lambda i: (i, 0)),
                   pl.BlockSpec((slabs, tm, LANES), lambda i: (0, i, 0)),
                   pl.BlockSpec((tm, w_dt.shape[1]), lambda i: (i, 0))],
        scratch_shapes=[pltpu.VMEM((tm + 2 * HALO, d), BF16)],
        compiler_params=_cparams("parallel"),
        name="ssd_inproj",
    )(h, h, h, g, mod, w_main, w_dt, conv_w, conv_b)


def _ssd_kernel(x_ref, b_ref, c_ref, dt_ref, bias_ref, alog_ref, s0_ref, *refs,
                reverse, finish, heads, col_off):
    if finish:
        yf_ref, z_ref, d_ref, nw_ref, o_ref, sfin_ref, state, y_scr = refs
    else:
        o_ref, sfin_ref, state = refs
        y_scr = o_ref
    L = SSD_CHUNK
    hd = SSD_HEADDIM
    hpg = heads // SSD_GROUPS
    gw = hpg * hd
    pairs = gw // LANES
    j = pl.program_id(1)

    @pl.when(j == 0)
    def _():
        state[...] = s0_ref[...]

    row = lax.broadcasted_iota(jnp.int32, (L, L), 0)
    col = lax.broadcasted_iota(jnp.int32, (L, L), 1)
    mask = (col >= row) if reverse else (col <= row)
    tri = jnp.where(mask, 1.0, 0.0).astype(BF16)
    first = col < hd
    first_row = lax.broadcasted_iota(jnp.int32, (1, LANES), 1) < hd

    v = dt_ref[...] + bias_ref[...]
    dt = jnp.maximum(v, 0.0) + jnp.log1p(jnp.exp(-jnp.abs(v)))
    dta = dt * (-jnp.exp(alog_ref[...]))
    p1, p2, p3 = _split3(dta)
    dot = functools.partial(jnp.dot, preferred_element_type=F32)
    dotm = lambda a, b: lax.dot_general(a, b, (((1,), (0,)), ((), ())), preferred_element_type=F32)
    a_cum = dot(tri, p1) + (dot(tri, p2) + dot(tri, p3))
    b_log = a_cum - jnp.log(dt)
    a_tot = a_cum[0:1, :] if reverse else a_cum[L - 1:L, :]
    a_cum2 = a_cum * LOG2E
    b_t = (b_log * LOG2E).T
    dtw_t = jnp.exp(a_tot - b_log).T
    e_tot = jnp.exp(a_tot)

    for g in range(SSD_GROUPS):
        bg = b_ref[g]
        cg = c_ref[g].astype(BF16)
        cb = lax.dot_general(cg, bg.astype(BF16), (((1,), (1,)), ((), ())), preferred_element_type=F32)
        bg_t = bg.T
        y_in = dot(cg, state[:, g * gw:(g + 1) * gw].astype(BF16))
        for k in range(pairs):
            ca = col_off + g * hpg + 2 * k
            lo = g * gw + k * LANES
            xp = x_ref[g * pairs + k]
            x_ab = jnp.concatenate([jnp.where(first, xp, 0.0), jnp.where(first, 0.0, xp)], axis=0).astype(BF16)
            a_a = jnp.broadcast_to(a_cum2[:, ca:ca + 1], (L, L))
            a_b = jnp.broadcast_to(a_cum2[:, ca + 1:ca + 2], (L, L))
            m_a = cb * jnp.exp2(jnp.where(mask, a_a - b_t[ca:ca + 1, :], -jnp.inf))
            m_b = cb * jnp.exp2(jnp.where(mask, a_b - b_t[ca + 1:ca + 2, :], -jnp.inf))
            e_cum = jnp.exp2(jnp.where(first, a_a, a_b))
            y_pair = dotm(jnp.concatenate([m_a, m_b], axis=1), x_ab) + y_in[:, k * LANES:(k + 1) * LANES] * e_cum
            if finish:
                y_pair = y_pair + xp * d_ref[:, lo:lo + LANES]
            y_scr[:, lo:lo + LANES] = y_pair
            bt_a = bg_t * dtw_t[ca:ca + 1, :]
            bt_b = bg_t * dtw_t[ca + 1:ca + 2, :]
            e_end = jnp.where(first_row, e_tot[:, ca:ca + 1], e_tot[:, ca + 1:ca + 2])
            state[:, lo:lo + LANES] = (state[:, lo:lo + LANES] * e_end
                                       + dotm(jnp.concatenate([bt_a, bt_b], axis=1), x_ab))

    if finish:
        y = y_scr[...] + yf_ref[...]
        gated = y * _silu(z_ref[...])
        gn = gated * lax.rsqrt(jnp.mean(gated * gated, axis=-1, keepdims=True) + RMS_EPS) * nw_ref[...]
        o_ref[...] = gn.astype(o_ref.dtype)

    @pl.when(j == pl.num_programs(1) - 1)
    def _():
        sfin_ref[...] = state[...]


def _ssd_scan(xc, dt_raw, dt_bias, a_log, state0, *, batch, seq, reverse, d_inner, finish_args=None):
    t = xc.shape[1]
    L = SSD_CHUNK
    nc = seq // L
    heads = d_inner // SSD_HEADDIM
    x_slabs = d_inner // LANES
    gs = SSD_GROUPS * SSD_STATE // LANES
    assert SSD_STATE == LANES and x_slabs % gs == 0
    finish = finish_args is not None

    def rows(b, j):
        return b * nc + ((nc - 1 - j) if reverse else j)

    in_specs = [
        pl.BlockSpec((x_slabs, L, LANES), lambda b, j: (0, rows(b, j), 0)),
        pl.BlockSpec((gs, L, LANES), lambda b, j: (x_slabs // gs, rows(b, j), 0)),
        pl.BlockSpec((gs, L, LANES), lambda b, j: (x_slabs // gs + 1, rows(b, j), 0)),
        pl.BlockSpec((L, LANES), lambda b, j: (rows(b, j), 0)),
        pl.BlockSpec((1, LANES), lambda b, j: (0, 0)),
        pl.BlockSpec((1, LANES), lambda b, j: (0, 0)),
        pl.BlockSpec((None, SSD_STATE, d_inner), lambda b, j: (b, 0, 0)),
    ]
    args = [xc, xc, xc, dt_raw, dt_bias, a_log, state0]
    if finish:
        y_f, z, d_exp, norm_w = finish_args
        in_specs += [
            pl.BlockSpec((L, d_inner), lambda b, j: (rows(b, j), 0)),
            pl.BlockSpec((L, d_inner), lambda b, j: (rows(b, j), 0)),
            pl.BlockSpec((1, d_inner), lambda b, j: (0, 0)),
            pl.BlockSpec((1, d_inner), lambda b, j: (0, 0)),
        ]
        args += [y_f, z, d_exp, norm_w]
    out_dtype = BF16 if finish else F32
    return pl.pallas_call(
        functools.partial(_ssd_kernel, reverse=reverse, finish=finish, heads=heads,
                          col_off=heads if reverse else 0),
        out_shape=[jax.ShapeDtypeStruct((t, d_inner), out_dtype),
                   jax.ShapeDtypeStruct((batch, SSD_STATE, d_inner), F32)],
        grid=(batch, nc),
        in_specs=in_specs,
        out_specs=[pl.BlockSpec((L, d_inner), lambda b, j: (rows(b, j), 0)),
                   pl.BlockSpec((None, SSD_STATE, d_inner), lambda b, j: (b, 0, 0))],
        scratch_shapes=[pltpu.VMEM((SSD_STATE, d_inner), F32)]
        + ([pltpu.VMEM((L, d_inner), F32)] if finish else []),
        compiler_params=_cparams("parallel", "arbitrary"),
        name="ssd_scan_bwd" if reverse else "ssd_scan_fwd",
    )(*args)


def _attn_kernel(sink_ref, q_ref, kvc_ref, wo_ref, h_ref, gate_ref, *refs, band, q_heads):
    if band:
        kvp_ref, kv0_ref, kvn_ref, o_ref = refs
    else:
        (o_ref,) = refs
    hd = ATTN_HEAD_DIM
    kd = ATTN_KV_HEADS * hd
    group = q_heads // ATTN_KV_HEADS
    blk = q_ref.shape[0]
    dot = functools.partial(jnp.dot, preferred_element_type=F32)
    dot_t = lambda a, b: lax.dot_general(a, b, (((1,), (1,)), ((), ())), preferred_element_type=F32)
    gq = group * blk
    key_idx = lax.broadcasted_iota(jnp.int32, (blk, gq), 0)
    qry_idx = lax.broadcasted_iota(jnp.int32, (blk, gq), 1) % blk
    lane_head = lax.broadcasted_iota(jnp.int32, (1, gq), 1) // blk
    segs = [(kvc_ref[c0:c0 + blk, :kd], kvc_ref[c0:c0 + blk, kd:], None) for c0 in range(0, kvc_ref.shape[0], blk)]
    if band:
        j = pl.program_id(1)
        ok_prev = key_idx >= qry_idx + jnp.where(j > 0, 0, 2 * blk)
        ok_next = key_idx + jnp.where(j < pl.num_programs(1) - 1, 0, 2 * blk) <= qry_idx
        segs += [(kvp_ref[:, :kd], kvp_ref[:, kd:], ok_prev), (kv0_ref[:, :kd], kv0_ref[:, kd:], None),
                 (kvn_ref[:, :kd], kvn_ref[:, kd:], ok_next)]
    v_t = [v.astype(F32).T.astype(BF16) for _, v, _ in segs]
    o_rows = []
    for kv in range(ATTN_KV_HEADS):
        ksl = slice(kv * hd, (kv + 1) * hd)
        h0 = kv * group
        qs = jnp.concatenate([q_ref[:, (h0 + i) * hd:(h0 + i + 1) * hd] for i in range(group)], axis=0)
        sink = jnp.full((1, gq), sink_ref[h0], F32)
        for i in range(1, group):
            sink = jnp.where(lane_head == i, sink_ref[h0 + i], sink)
        s_tiles = []
        for k, _, ok in segs:
            s = dot_t(k[:, ksl], qs)
            s_tiles.append(s if ok is None else jnp.where(ok, s, -jnp.inf))
        m = jnp.maximum(jnp.max(functools.reduce(jnp.maximum, s_tiles), axis=0, keepdims=True), sink)
        p_tiles = [jnp.exp(s - m) for s in s_tiles]
        den = jnp.exp(sink - m) + jnp.sum(functools.reduce(jnp.add, p_tiles), axis=0, keepdims=True)
        o_t = functools.reduce(jnp.add, [dot(vt[ksl, :], p.astype(BF16)) for vt, p in zip(v_t, p_tiles)])
        o_t = o_t * (1.0 / den)
        o_rows += [o_t[:, i * blk:(i + 1) * blk] for i in range(group)]
    o = jnp.concatenate(o_rows, axis=0).T.astype(BF16)
    o_ref[...] = h_ref[...] + gate_ref[...] * dot(o, wo_ref[...])


def _attention(sink, qkv, qkv_ctx, w_o, h, gate, *, batch, seq, n_ctx, q_heads, band):
    blk = ATTN_BLOCK
    assert n_ctx % blk == 0 and seq % blk == 0
    nb = seq // blk
    d = h.shape[1]
    qd = q_heads * ATTN_HEAD_DIM
    kd = ATTN_KV_HEADS * ATTN_HEAD_DIM
    assert qd % (2 * kd) == 0
    kvcol = qd // (2 * kd)
    gate_idx = (lambda b, j: (b, 0, 0)) if gate.shape[0] > 1 else (lambda b, j: (0, 0, 0))
    in_specs = [
        pl.BlockSpec(memory_space=pltpu.SMEM),
        pl.BlockSpec((blk, qd), lambda b, j: (b * nb + j, 0)),
        pl.BlockSpec((n_ctx, 2 * kd), lambda b, j: (b, kvcol)),
        pl.BlockSpec(w_o.shape, lambda b, j: (0, 0), pipeline_mode=pl.Buffered(1)),
        pl.BlockSpec((blk, d), lambda b, j: (b * nb + j, 0)),
        pl.BlockSpec((None, 1, d), gate_idx),
    ]
    args = [sink, qkv, qkv_ctx, w_o, h, gate]
    if band:
        prev = lambda b, j: b * nb + jnp.maximum(j - 1, 0)
        cur = lambda b, j: b * nb + j
        nxt = lambda b, j: b * nb + jnp.minimum(j + 1, nb - 1)
        for f in (prev, cur, nxt):
            in_specs.append(pl.BlockSpec((blk, 2 * kd), functools.partial(lambda b, j, f: (f(b, j), kvcol), f=f)))
            args.append(qkv)
    return pl.pallas_call(
        functools.partial(_attn_kernel, band=band, q_heads=q_heads),
        out_shape=jax.ShapeDtypeStruct((batch * seq, d), F32),
        grid=(batch, nb),
        in_specs=in_specs,
        out_specs=pl.BlockSpec((blk, d), lambda b, j: (b * nb + j, 0)),
        compiler_params=_cparams("parallel", "parallel"),
        name="window_attn" if band else "ctx_attn",
    )(*args)


def _gmlp_kernel(x_ref, g_ref, mod_ref, win_ref, lg_ref, lb_ref, ws_ref, bs_ref, wo_ref, o_ref, gm_scr, *, chunk):
    x = x_ref[...]
    a = _norm_mod(x, g_ref[...], mod_ref[0:1, :], mod_ref[1:2, :]).astype(BF16)
    width = wo_ref.shape[0]
    dot = functools.partial(jnp.dot, preferred_element_type=F32)

    def gelu_proj(c0):
        y = dot(a, win_ref[:, c0:c0 + chunk])
        return 0.5 * y * (1.0 + lax.erf(y * (1.0 / math.sqrt(2.0))))

    v = jnp.concatenate([gelu_proj(width + c0) for c0 in range(0, width, chunk)], axis=1)
    mu = jnp.mean(v, axis=-1, keepdims=True)
    vc = v - mu
    var = jnp.mean(vc * vc, axis=-1, keepdims=True)
    vn = (vc * lax.rsqrt(var + LN_EPS) * lg_ref[...] + lb_ref[...]).astype(BF16)
    gd = width // GMLP_GROUPS
    ch = GMLP_CHUNK
    for c0 in range(0, width, chunk):
        u = gelu_proj(c0)
        for g in range(c0 // gd, (c0 + chunk) // gd):
            for r in range(0, x.shape[0], ch):
                sv = dot(ws_ref[g], vn[r:r + ch, g * gd:(g + 1) * gd]) + bs_ref[:, g:g + 1]
                gm_scr[r:r + ch, g * gd:(g + 1) * gd] = (u[r:r + ch, g * gd - c0:(g + 1) * gd - c0] * sv).astype(BF16)
    o_ref[...] = x + mod_ref[2:3, :] * dot(gm_scr[...], wo_ref[...])


def _gmlp(h, g, mod, w_in, ln_g, ln_b, w_s, b_s_t, w_out, *, seq, tm, chunk):
    t, d = h.shape
    width = w_out.shape[0]
    assert tm % GMLP_CHUNK == 0 and chunk % (width // GMLP_GROUPS) == 0
    tiles_per_seq = seq // tm
    bm = mod.shape[0]
    mod_idx = (lambda i: (i // tiles_per_seq, 0, 0)) if bm > 1 else (lambda i: (0, 0, 0))
    resident = pl.Buffered(1)
    return pl.pallas_call(
        functools.partial(_gmlp_kernel, chunk=chunk),
        out_shape=jax.ShapeDtypeStruct((t, d), F32),
        grid=(t // tm,),
        in_specs=[
            pl.BlockSpec((tm, d), lambda i: (i, 0)),
            pl.BlockSpec((1, d), lambda i: (0, 0)),
            pl.BlockSpec((None, 3, d), mod_idx),
            pl.BlockSpec(w_in.shape, lambda i: (0, 0), pipeline_mode=resident),
            pl.BlockSpec((1, width), lambda i: (0, 0)),
            pl.BlockSpec((1, width), lambda i: (0, 0)),
            pl.BlockSpec(w_s.shape, lambda i: (0, 0, 0)),
            pl.BlockSpec(b_s_t.shape, lambda i: (0, 0)),
            pl.BlockSpec(w_out.shape, lambda i: (0, 0), pipeline_mode=resident),
        ],
        out_specs=pl.BlockSpec((tm, d), lambda i: (i, 0)),
        scratch_shapes=[pltpu.VMEM((tm, width), BF16)],
        compiler_params=_cparams("parallel"),
        name="gmlp",
    )(h, g, mod, w_in, ln_g, ln_b, w_s, b_s_t, w_out)


def _rope_tables(n):
    freqs = ATTN_HEAD_DIM // 4
    rows = n // GRID_W
    row = jnp.repeat(jnp.arange(rows, dtype=jnp.int32), GRID_W, total_repeat_length=n)
    col = jnp.tile(jnp.arange(GRID_W, dtype=jnp.int32), rows)
    inv = ROPE_BASE ** (-jnp.arange(freqs, dtype=F32) / freqs)
    ang = jnp.stack([row.astype(F32)[:, None] * inv, col.astype(F32)[:, None] * inv], axis=1)
    ang = jnp.repeat(ang[:, :, None, :], 2, axis=2).reshape(n, ATTN_HEAD_DIM)
    cos, sin = jnp.cos(ang), jnp.sin(ang)
    reps = LANES // ATTN_HEAD_DIM
    cos, sin = jnp.tile(cos, (1, reps)), jnp.tile(sin, (1, reps))
    first_half = jnp.asarray((np.arange(LANES) // freqs) % 2 == 0)[None, :]
    return cos, jnp.where(first_half, -sin, 0.0), jnp.where(first_half, 0.0, sin)


def _ssd_mixer(streams, norm_g, w_in, conv_w, conv_b, a_log, dt_bias, d_skip, norm_w, w_out, need_ctx_out):
    d_inner = norm_w.shape[0]
    conv_dim = conv_w.shape[1]
    heads = d_inner // SSD_HEADDIM
    w_main = w_in[:, :d_inner + conv_dim].astype(BF16)
    w_dt = jnp.pad(w_in[:, d_inner + conv_dim:], ((0, 0), (0, LANES - 2 * heads))).astype(BF16)
    bias = jnp.pad(dt_bias.reshape(1, 2 * heads), ((0, 0), (0, LANES - 2 * heads)))
    alog = jnp.pad(a_log.reshape(1, 2 * heads), ((0, 0), (0, LANES - 2 * heads)))
    d_exp = jnp.repeat(d_skip, SSD_HEADDIM).reshape(1, d_inner)
    nw = norm_w.reshape(1, d_inner)
    w_out = w_out.astype(BF16)
    slabs = conv_dim // LANES
    conv_w = jnp.pad(conv_w, ((0, SUBLANES - SSD_CONV_W), (0, 0))).reshape(SUBLANES, slabs, LANES).transpose(1, 0, 2)
    conv_b = conv_b.reshape(slabs, 1, LANES)

    outs = []
    s_f = s_b = None
    for name, s in streams:
        if s_f is None:
            s_f = s_b = jnp.zeros((s["batch"], SSD_STATE, d_inner), F32)
        z, xc, dt_raw = _ssd_inproj(s["h"], norm_g, s["mod"][:, 0:2], w_main, w_dt, conv_w, conv_b,
                                    seq=s["seq"], tm=s["tm"], d_inner=d_inner, chunk=1024)
        y_f, s_f = _ssd_scan(xc, dt_raw, bias, alog, s_f, batch=s["batch"], seq=s["seq"], reverse=False,
                             d_inner=d_inner)
        gn, s_b = _ssd_scan(xc, dt_raw, bias, alog, s_b, batch=s["batch"], seq=s["seq"], reverse=True,
                            d_inner=d_inner, finish_args=(y_f, z, d_exp, nw))
        if name == "ctx" and not need_ctx_out:
            outs.append(None)
        else:
            tm_out = 2 * s["tm"] if s["seq"] % (2 * s["tm"]) == 0 else s["tm"]
            outs.append(_outproj(gn, w_out, s["h"], s["mod"][:, 2:3], seq=s["seq"], tm=tm_out))
    return outs


def _attn_mixer(streams, norm_g, w_qkv, sink, w_o, tables, need_ctx_out):
    (_, sc), (_, sl) = streams
    q_heads = sink.shape[0]
    qd = q_heads * ATTN_HEAD_DIM
    kd = ATTN_KV_HEADS * ATTN_HEAD_DIM
    w_qkv = w_qkv.astype(BF16)
    w_o = w_o.astype(BF16)
    qkv_c = _qkv_proj(sc["h"], norm_g, sc["mod"][:, 0:2], w_qkv, None, seq=sc["seq"], tm=sc["tm"],
                      q_cols=qd, k_cols=kd)
    qkv_l = _qkv_proj(sl["h"], norm_g, sl["mod"][:, 0:2], w_qkv, tables, seq=sl["seq"], tm=sl["tm"],
                      q_cols=qd, k_cols=kd)
    h_l = _attention(sink, qkv_l, qkv_c, w_o, sl["h"], sl["mod"][:, 2:3], batch=sl["batch"], seq=sl["seq"],
                     n_ctx=sc["seq"], q_heads=q_heads, band=True)
    h_c = None
    if need_ctx_out:
        h_c = _attention(sink, qkv_c, qkv_c, w_o, sc["h"], sc["mod"][:, 2:3], batch=sc["batch"], seq=sc["seq"],
                         n_ctx=sc["seq"], q_heads=q_heads, band=False)
    return [h_c, h_l]


def _gmlp_mixer(streams, norm_g, w_in, ln_g, ln_b, w_s, b_s, w_out, need_ctx_out):
    width = ln_g.shape[0]
    w_in = w_in.astype(BF16)
    w_out = w_out.astype(BF16)
    w_s = w_s.astype(BF16)
    outs = []
    for name, s in streams:
        if name == "ctx" and not need_ctx_out:
            outs.append(None)
            continue
        outs.append(_gmlp(s["h"], norm_g, s["mod"][:, 0:3], w_in, ln_g.reshape(1, width), ln_b.reshape(1, width),
                          w_s, b_s.T, w_out, seq=s["seq"], tm=s["tm"], chunk=1024))
    return outs


def kernel(x, c, ctx, c_ctx, w_mod, b_mod, norm_g, final_g, ssd_w_in, ssd_conv_w, ssd_conv_b, ssd_a_log, ssd_dt_bias, ssd_d, ssd_norm_w, ssd_w_out, attn_w_qkv, attn_sink, attn_w_o, gmlp_w_in, gmlp_ln_g, gmlp_ln_b, gmlp_w_s, gmlp_b_s, gmlp_w_out, ffn_w_in, ffn_w_out):
    batch, seq, d = x.shape
    n_ctx = ctx.shape[1]
    depth = w_mod.shape[0]
    hidden = ffn_w_out.shape[1]

    pad_rows = -(batch + 1) % 16
    cc = jnp.concatenate([c, c_ctx[None, :], jnp.zeros((pad_rows, d), F32)], axis=0)
    mods = _modulation(cc, w_mod, b_mod)
    tables = _rope_tables(seq)

    h_lat = x.reshape(batch * seq, d)
    h_ctx = ctx.reshape(batch * n_ctx, d)
    tm_lat = 512 if seq % 512 == 0 else 256
    tm_ctx = 256
    ffn_chunk = hidden

    for i in range(depth):
        last = i == depth - 1
        kind, j = i % 3, i // 3
        mod_lat = mods[i, :batch].reshape(batch, N_MOD, d)
        mod_ctx = mods[i, batch:batch + 1].reshape(1, N_MOD, d)
        g1 = norm_g[i, 0].reshape(1, d)
        g2 = norm_g[i, 1].reshape(1, d)
        streams = [
            ("ctx", dict(h=h_ctx, mod=mod_ctx, batch=batch, seq=n_ctx, tm=tm_ctx)),
            ("lat", dict(h=h_lat, mod=mod_lat, batch=batch, seq=seq, tm=tm_lat)),
        ]
        need_ctx_out = not last
        if kind == 0:
            h_ctx_new, h_lat = _ssd_mixer(streams, g1, ssd_w_in[j], ssd_conv_w[j], ssd_conv_b[j], ssd_a_log[j],
                                          ssd_dt_bias[j], ssd_d[j], ssd_norm_w[j], ssd_w_out[j], need_ctx_out)
        elif kind == 1:
            h_ctx_new, h_lat = _attn_mixer(streams, g1, attn_w_qkv[j], attn_sink[j], attn_w_o[j], tables,
                                           need_ctx_out)
        else:
            h_ctx_new, h_lat = _gmlp_mixer(streams, g1, gmlp_w_in[j], gmlp_ln_g[j], gmlp_ln_b[j], gmlp_w_s[j],
                                           gmlp_b_s[j], gmlp_w_out[j], need_ctx_out)

        w_in = ffn_w_in[i].astype(BF16)
        w_out = ffn_w_out[i].astype(BF16)
        h_lat = _ffn(h_lat, g2, mod_lat[:, 3:6], w_in, w_out, seq=seq, tm=tm_lat, chunk=ffn_chunk,
                     final_g=final_g.reshape(1, d) if last else None)
        if need_ctx_out:
            h_ctx = _ffn(h_ctx_new, g2, mod_ctx[:, 3:6], w_in, w_out, seq=n_ctx, tm=tm_ctx, chunk=ffn_chunk)
    return h_lat.reshape(batch, seq, d)
```
